```python
import math
import jax, jax.numpy as jnp
from jax import lax
import numpy as np

D_MODEL = 1024
BATCH = 1
SEQ = 16384
DEPTH = 1

SSD_EXPAND = 2
SSD_INNER = SSD_EXPAND * D_MODEL
SSD_HEAD_DIM = 64
SSD_HEADS = SSD_INNER // SSD_HEAD_DIM
SSD_GROUPS = 4
SSD_HEADS_PER_GROUP = SSD_HEADS // SSD_GROUPS
SSD_STATE = 128
SSD_CONV = 4
SSD_CHUNK = 128
SSD_CONV_DIM = SSD_INNER + 2 * SSD_GROUPS * SSD_STATE

POOL_WINDOWS = (2, 4, 8, 16)
POOL_GROUPS = len(POOL_WINDOWS)
POOL_WIDTH = D_MODEL
POOL_GROUP_DIM = POOL_WIDTH // POOL_GROUPS

FFN_HIDDEN = 2816
FFN_CONV = 3

N_BRANCHES = 2
N_MOD = 6
EPS = 1e-6

IN_WIDTH = SSD_INNER + SSD_CONV_DIM + SSD_HEADS + POOL_WIDTH + N_BRANCHES * D_MODEL

kernel_name = "hybrid_ssd_pool_convglu_adaln_block"


def rmsnorm(x, g):
    xf = x.astype(jnp.float32)
    y = xf * lax.rsqrt(jnp.mean(xf * xf, axis=-1, keepdims=True) + EPS)
    return (y * g.astype(jnp.float32)).astype(x.dtype)


def group_rmsnorm(x, g, n_groups):
    b, s, d = x.shape
    xf = x.astype(jnp.float32).reshape(b, s, n_groups, d // n_groups)
    y = xf * lax.rsqrt(jnp.mean(xf * xf, axis=-1, keepdims=True) + EPS)
    return (y.reshape(b, s, d) * g.astype(jnp.float32)).astype(x.dtype)


def causal_dwconv(u, w, b):
    k_width = w.shape[0]
    s = u.shape[1]
    up = jnp.pad(u, ((0, 0), (k_width - 1, 0), (0, 0)))
    y = b + w[0] * up[:, 0:s]
    for k in range(1, k_width):
        y = y + w[k] * up[:, k:k + s]
    return y


def ssd_chunked(xh, dt, a, bm, cm):
    bsz, s, _, p = xh.shape
    nc = s // SSD_CHUNK
    L, G, K, N = SSD_CHUNK, SSD_GROUPS, SSD_HEADS_PER_GROUP, SSD_STATE
    xdt = (xh * dt[..., None]).reshape(bsz, nc, L, G, K, p)
    da = (dt * a).reshape(bsz, nc, L, G, K)
    bc = bm.reshape(bsz, nc, L, G, N)
    cc = cm.reshape(bsz, nc, L, G, N)
    xs = (jnp.moveaxis(xdt, 1, 0), jnp.moveaxis(da, 1, 0), jnp.moveaxis(bc, 1, 0), jnp.moveaxis(cc, 1, 0))
    causal = jnp.tril(jnp.ones((L, L), dtype=bool))[None, :, :, None, None]

    def step(state, inp):
        xdt_c, da_c, b_c, c_c = inp
        acum = jnp.cumsum(da_c, axis=1)
        seg = acum[:, :, None] - acum[:, None, :]
        decay = jnp.exp(jnp.where(causal, seg, -jnp.inf))
        cb = jnp.einsum('blgn,bsgn->blsg', c_c, b_c)
        y_diag = jnp.einsum('blsg,blsgk,bsgkp->blgkp', cb, decay, xdt_c)
        y_off = jnp.einsum('blgn,bgkpn,blgk->blgkp', c_c, state, jnp.exp(acum))
        decay_to_end = jnp.exp(acum[:, -1:] - acum)
        new_state = state * jnp.exp(acum[:, -1])[..., None, None] + \
            jnp.einsum('bsgn,bsgk,bsgkp->bgkpn', b_c, decay_to_end, xdt_c)
        return new_state, y_diag + y_off

    state0 = jnp.zeros((bsz, G, K, p, N), jnp.float32)
    _, ys = lax.scan(step, state0, xs)
    return jnp.moveaxis(ys, 0, 1).reshape(bsz, s, SSD_HEADS, p)


def multiscale_pool(p):
    bsz, s, _ = p.shape
    pg = p.astype(jnp.float32).reshape(bsz, s, POOL_GROUPS, POOL_GROUP_DIM)
    cs = jnp.pad(jnp.cumsum(pg, axis=1), ((0, 0), (1, 0), (0, 0), (0, 0)))
    t = jnp.arange(1, s + 1)
    outs = []
    for g, w in enumerate(POOL_WINDOWS):
        lo = jnp.maximum(t - w, 0)
        win_sum = cs[:, 1:, g] - cs[:, lo, g]
        cnt = jnp.minimum(t, w).astype(jnp.float32)
        outs.append(win_sum / cnt[None, :, None] - pg[:, :, g])
    return jnp.stack(outs, axis=2).astype(p.dtype)


def hybrid_mixer(u, w_in, conv_w, conv_b, dt_bias, a_log, d_skip, g_ssd_norm, w_ssd_out,
                 w_pool_grp, pool_scale, w_pool_out, w_out):
    bsz, s, _ = u.shape
    o1 = SSD_INNER
    o2 = o1 + SSD_CONV_DIM
    o3 = o2 + SSD_HEADS
    o4 = o3 + POOL_WIDTH
    proj = u @ w_in
    z, xbc, dt_raw, p_in, gates = jnp.split(proj, [o1, o2, o3, o4], axis=-1)

    xbc = jax.nn.silu(causal_dwconv(xbc, conv_w, conv_b))
    xs, bm, cm = jnp.split(xbc, [SSD_INNER, SSD_INNER + SSD_GROUPS * SSD_STATE], axis=-1)
    xh = xs.reshape(bsz, s, SSD_HEADS, SSD_HEAD_DIM)
    dt = jax.nn.softplus((dt_raw + dt_bias).astype(jnp.float32))
    a = -jnp.exp(a_log.astype(jnp.float32))
    y = ssd_chunked(xh.astype(jnp.float32), dt, a,
                    bm.reshape(bsz, s, SSD_GROUPS, SSD_STATE).astype(jnp.float32),
                    cm.reshape(bsz, s, SSD_GROUPS, SSD_STATE).astype(jnp.float32))
    y = y + d_skip.astype(jnp.float32)[:, None] * xh.astype(jnp.float32)
    y = y.reshape(bsz, s, SSD_INNER).astype(u.dtype)
    y = group_rmsnorm(y * jax.nn.silu(z), g_ssd_norm, SSD_GROUPS)
    y_a = y @ w_ssd_out

    pb = multiscale_pool(p_in)
    pb = jnp.einsum('bsgc,gcd->bsgd', pb, w_pool_grp).reshape(bsz, s, POOL_WIDTH) * pool_scale
    y_b = pb @ w_pool_out

    g_a, g_b = jnp.split(jax.nn.sigmoid(gates), 2, axis=-1)
    return (g_a * y_a + g_b * y_b) @ w_out


def conv_glu(v, w_ffn_in, conv_w, conv_b, w_ffn_out):
    val, gate = jnp.split(v @ w_ffn_in, 2, axis=-1)
    gate = causal_dwconv(gate, conv_w, conv_b)
    return (jax.nn.silu(gate) * val) @ w_ffn_out


def setup_inputs(seed: int = 0) -> dict:
    key = jax.random.key(seed)
    ks = jax.random.split(key, 24)
    f32 = jnp.float32
    nrm = lambda k, shape, scale: jax.random.normal(k, shape, f32) * scale
    dt0 = jnp.exp(jax.random.uniform(ks[8], (DEPTH, SSD_HEADS), f32, math.log(1e-3), math.log(1e-1)))
    return {
        "x": nrm(ks[0], (BATCH, SEQ, D_MODEL), 1.0),
        "c": nrm(ks[1], (BATCH, D_MODEL), 1.0),
        "w_ada": nrm(ks[2], (DEPTH, D_MODEL, N_MOD * D_MODEL), 0.5 * D_MODEL ** -0.5),
        "b_ada": nrm(ks[3], (DEPTH, N_MOD * D_MODEL), 0.02),
        "g_norm1": 1.0 + nrm(ks[4], (DEPTH, D_MODEL), 0.02),
        "w_in": nrm(ks[5], (DEPTH, D_MODEL, IN_WIDTH), D_MODEL ** -0.5),
        "ssd_conv_w": nrm(ks[6], (DEPTH, SSD_CONV, SSD_CONV_DIM), SSD_CONV ** -0.5),
        "ssd_conv_b": nrm(ks[7], (DEPTH, SSD_CONV_DIM), 0.02),
        "ssd_dt_bias": dt0 + jnp.log(-jnp.expm1(-dt0)),
        "ssd_a_log": jnp.log(jax.random.uniform(ks[9], (DEPTH, SSD_HEADS), f32, 1.0, 16.0)),
        "ssd_d": 1.0 + nrm(ks[10], (DEPTH, SSD_HEADS), 0.02),
        "g_ssd_norm": 1.0 + nrm(ks[11], (DEPTH, SSD_INNER), 0.02),
        "w_ssd_out": nrm(ks[12], (DEPTH, SSD_INNER, D_MODEL), SSD_INNER ** -0.5),
        "w_pool_grp": nrm(ks[13], (DEPTH, POOL_GROUPS, POOL_GROUP_DIM, POOL_GROUP_DIM), POOL_GROUP_DIM ** -0.5),
        "pool_scale": 1.0 + nrm(ks[14], (DEPTH, POOL_WIDTH), 0.02),
        "w_pool_out": nrm(ks[15], (DEPTH, POOL_WIDTH, D_MODEL), POOL_WIDTH ** -0.5),
        "w_out": nrm(ks[16], (DEPTH, D_MODEL, D_MODEL), D_MODEL ** -0.5),
        "g_norm2": 1.0 + nrm(ks[17], (DEPTH, D_MODEL), 0.02),
        "w_ffn_in": nrm(ks[18], (DEPTH, D_MODEL, 2 * FFN_HIDDEN), D_MODEL ** -0.5),
        "ffn_conv_w": nrm(ks[19], (DEPTH, FFN_CONV, FFN_HIDDEN), FFN_CONV ** -0.5),
        "ffn_conv_b": nrm(ks[20], (DEPTH, FFN_HIDDEN), 0.02),
        "w_ffn_out": nrm(ks[21], (DEPTH, FFN_HIDDEN, D_MODEL), FFN_HIDDEN ** -0.5),
        "g_final": 1.0 + nrm(ks[22], (D_MODEL,), 0.02),
    }


def reference(x, c, w_ada, b_ada, g_norm1, w_in, ssd_conv_w, ssd_conv_b, ssd_dt_bias, ssd_a_log,
              ssd_d, g_ssd_norm, w_ssd_out, w_pool_grp, pool_scale, w_pool_out, w_out, g_norm2,
              w_ffn_in, ffn_conv_w, ffn_conv_b, w_ffn_out, g_final):
    h = x
    cond = jax.nn.silu(c)
    for layer in range(DEPTH):
        mod = cond @ w_ada[layer] + b_ada[layer]
        shift1, scale1, gate1, shift2, scale2, gate2 = jnp.split(mod[:, None, :], N_MOD, axis=-1)
        u = rmsnorm(h, g_norm1[layer]) * (1.0 + scale1) + shift1
        h = h + gate1 * hybrid_mixer(u, w_in[layer], ssd_conv_w[layer], ssd_conv_b[layer],
                                     ssd_dt_bias[layer], ssd_a_log[layer], ssd_d[layer],
                                     g_ssd_norm[layer], w_ssd_out[layer], w_pool_grp[layer],
                                     pool_scale[layer], w_pool_out[layer], w_out[layer])
        v = rmsnorm(h, g_norm2[layer]) * (1.0 + scale2) + shift2
        h = h + gate2 * conv_glu(v, w_ffn_in[layer], ffn_conv_w[layer], ffn_conv_b[layer],
                                 w_ffn_out[layer])
    return rmsnorm(h, g_final)
```

```python
import functools

import jax
import jax.numpy as jnp
from jax import lax
from jax.experimental import pallas as pl
from jax.experimental.pallas import tpu as pltpu

F32 = jnp.float32
BF16 = jnp.bfloat16

EPS = 1e-6
SSD_HEAD_DIM = 64
SSD_GROUPS = 4
SSD_STATE = 128
SSD_CONV = 4
SSD_CHUNK = 128
POOL_WINDOWS = (2, 4, 8, 16)
POOL_HALO = 16
FFN_CONV = 3
N_MOD = 6
LANES = 128
SUBLANES = 8
VMEM_LIMIT = 56 * 1024 * 1024


def _dot(a, b):
    return jnp.dot(a, b, preferred_element_type=F32)


def _silu(v):
    return v * jax.nn.sigmoid(v)


def _resident(shape):
    return pl.BlockSpec(shape, lambda i: (0,) * len(shape), pipeline_mode=pl.Buffered(1))


def _rows(tile, width):
    return pl.BlockSpec((tile, width), lambda i: (i, 0))


def _params(vmem=VMEM_LIMIT):
    return pltpu.CompilerParams(dimension_semantics=("arbitrary",), vmem_limit_bytes=vmem)


def _ada_kernel(c_ref, w_ref, b_ref, o_ref):
    cond = _silu(c_ref[...])
    cond8 = jnp.broadcast_to(cond, (SUBLANES, cond.shape[1])).astype(BF16)
    o_ref[...] = _dot(cond8, w_ref[...].astype(BF16))[0:1] + b_ref[...]


def _ada(c, w_ada, b_ada):
    d, n = w_ada.shape
    blk = d
    return pl.pallas_call(
        _ada_kernel,
        grid=(n // blk,),
        in_specs=[pl.BlockSpec((1, d), lambda j: (0, 0)),
                  pl.BlockSpec((d, blk), lambda j: (0, j)),
                  pl.BlockSpec((1, blk), lambda j: (0, j))],
        out_specs=pl.BlockSpec((1, blk), lambda j: (0, j)),
        out_shape=jax.ShapeDtypeStruct((1, n), F32),
        compiler_params=_params(),
        name="ada",
    )(c, w_ada, b_ada.reshape(1, n))


def _modulated_rmsnorm(x, g, shift, scale):
    y = x * lax.rsqrt(jnp.mean(x * x, axis=-1, keepdims=True) + EPS) * g
    return y * (1.0 + scale) + shift


def _front_kernel(x_ref, mod_ref, g_ref, wz_ref, wx_ref, wdt_ref, wp_ref, wg_ref,
                  z_ref, xbc_ref, dt_ref, p_ref, gt_ref):
    u = _modulated_rmsnorm(x_ref[...], g_ref[...], mod_ref[0:1, :], mod_ref[1:2, :]).astype(BF16)
    z_ref[...] = _dot(u, wz_ref[...])
    xbc_ref[...] = _dot(u, wx_ref[...])
    dt_ref[...] = _dot(u, wdt_ref[...])
    p_ref[...] = _dot(u, wp_ref[...])
    gt_ref[...] = _dot(u, wg_ref[...])


def _front(x, mod, g1, wz, wx, wdt, wp, wg, tile):
    s, d = x.shape
    widths = [w.shape[1] for w in (wz, wx, wdt, wp, wg)]
    return pl.pallas_call(
        _front_kernel,
        grid=(s // tile,),
        in_specs=[_rows(tile, d), _resident(mod.shape), _resident(g1.shape)]
                 + [_resident(w.shape) for w in (wz, wx, wdt, wp, wg)],
        out_specs=[_rows(tile, n) for n in widths],
        out_shape=[jax.ShapeDtypeStruct((s, n), F32) for n in widths],
        compiler_params=_params(),
        name="front",
    )(x, mod, g1, wz, wx, wdt, wp, wg)


def _split3(v):
    hi = v.astype(BF16)
    r = v - hi.astype(F32)
    mid = r.astype(BF16)
    lo = (r - mid.astype(F32)).astype(BF16)
    return hi, mid, lo


def _ssd_kernel(xbc_ref, dt_ref, z_ref, cw_ref, cb_ref, dtb_ref, alog_ref, dsk_ref, gn_ref,
                yn_ref, xpad_ref, xc_ref, st_ref, y_ref, *, inner, heads):
    L = SSD_CHUNK
    N = SSD_STATE
    P = SSD_HEAD_DIM
    hpg = heads // SSD_GROUPS
    width = xbc_ref.shape[1]
    tail = SSD_CONV - 1

    @pl.when(pl.program_id(0) == 0)
    def _():
        xpad_ref[0:SUBLANES, :] = jnp.zeros((SUBLANES, width), F32)
        st_ref[...] = jnp.zeros(st_ref.shape, F32)

    xpad_ref[SUBLANES:SUBLANES + L, :] = xbc_ref[...]
    cblk = 4 * LANES
    for c0 in range(0, width, cblk):
        cs = slice(c0, c0 + cblk)
        acc = cb_ref[:, cs]
        for k in range(SSD_CONV):
            acc = acc + cw_ref[k:k + 1, cs] * xpad_ref[pl.ds(SUBLANES - tail + k, L), cs]
        xc_ref[:, cs] = _silu(acc)
    xpad_ref[0:SUBLANES, :] = xpad_ref[L:L + SUBLANES, :]

    dtv = jax.nn.softplus(dt_ref[...] + dtb_ref[...])
    da = dtv * (-jnp.exp(alog_ref[...]))
    row = lax.broadcasted_iota(jnp.int32, (L, L), 0)
    col = lax.broadcasted_iota(jnp.int32, (L, L), 1)
    causal = row >= col
    tril = jnp.where(causal, 1.0, 0.0).astype(BF16)
    hi, mid, lo = _split3(da)
    acum = _dot(tril, hi) + _dot(tril, mid) + _dot(tril, lo)
    acum_t = acum.T
    dt_t = dtv.T
    last = acum[L - 1:L, :]
    wgt = dtv * jnp.exp(last - acum)
    ea_last = jnp.exp(last)

    lane = lax.broadcasted_iota(jnp.int32, (L, 2 * LANES), 1)
    lane1 = lax.broadcasted_iota(jnp.int32, (L, LANES), 1)
    lane_row = lax.broadcasted_iota(jnp.int32, (1, LANES), 1)

    for g in range(SSD_GROUPS):
        b_g = xc_ref[:, inner + g * N: inner + (g + 1) * N]
        c_g = xc_ref[:, inner + (SSD_GROUPS + g) * N: inner + (SSD_GROUPS + g + 1) * N]
        cb = lax.dot_general(c_g.astype(BF16), b_g.astype(BF16),
                             (((1,), (1,)), ((), ())), preferred_element_type=F32)
        scale_rows = []
        for quad in range(hpg // 4):
            h0 = g * hpg + quad * 4
            q0 = h0 * P
            lhs = []
            for h in range(h0, h0 + 4):
                colb = jnp.broadcast_to(acum[:, h:h + 1], (L, L))
                seg = colb - acum_t[h:h + 1, :]
                dec = jnp.exp(jnp.where(causal, seg, -jnp.inf))
                w_h = cb * dec * dt_t[h:h + 1, :]
                e_h = c_g * jnp.exp(colb)
                lhs.append(jnp.concatenate([w_h.astype(BF16), e_h.astype(BF16)], axis=1))
                scale_rows.append(jnp.broadcast_to(ea_last[:, h:h + 1], (1, LANES)))
            lhs = jnp.concatenate(lhs, axis=0)
            xq = xc_ref[:, q0:q0 + 4 * P]
            rhs = jnp.concatenate([xq.astype(BF16), st_ref[:, q0:q0 + 4 * P].astype(BF16)], axis=0)
            out = _dot(lhs, rhs)
            yq = jnp.where(lane < P, out[0:L],
                           jnp.where(lane < 2 * P, out[L:2 * L],
                                     jnp.where(lane < 3 * P, out[2 * L:3 * L], out[3 * L:4 * L])))
            y_ref[:, q0:q0 + 4 * P] = yq + dsk_ref[:, q0:q0 + 4 * P] * xq
        xw = []
        for pair in range(hpg // 2):
            h0 = g * hpg + pair * 2
            p0 = h0 * P
            wb = jnp.where(lane1 < P,
                           jnp.broadcast_to(wgt[:, h0:h0 + 1], (L, LANES)),
                           jnp.broadcast_to(wgt[:, h0 + 1:h0 + 2], (L, LANES)))
            xw.append((xc_ref[:, p0:p0 + 2 * P] * wb).astype(BF16))
            srow = jnp.where(lane_row < P, scale_rows[pair * 2], scale_rows[pair * 2 + 1])
            st_ref[:, p0:p0 + 2 * P] = st_ref[:, p0:p0 + 2 * P] * srow
        xw = jnp.concatenate(xw, axis=1)
        g0 = g * hpg * P
        st_ref[:, g0:g0 + hpg * P] += _dot(b_g.T.astype(BF16), xw)

    gw = inner // SSD_GROUPS
    for g in range(SSD_GROUPS):
        gs = slice(g * gw, (g + 1) * gw)
        t = y_ref[:, gs] * _silu(z_ref[:, gs])
        ms = jnp.mean(t * t, axis=-1, keepdims=True)
        yn_ref[:, gs] = (t * lax.rsqrt(ms + EPS) * gn_ref[:, gs]).astype(yn_ref.dtype)


def _ssd(xbc, dt, z, conv_w, conv_b, dt_bias, a_log, d_skip, g_norm, heads):
    s, width = xbc.shape
    inner = z.shape[1]
    L = SSD_CHUNK
    kern = functools.partial(_ssd_kernel, inner=inner, heads=heads)
    small = (conv_w, conv_b, dt_bias, a_log, d_skip, g_norm)
    return pl.pallas_call(
        kern,
        grid=(s // L,),
        in_specs=[_rows(L, width), _rows(L, dt.shape[1]), _rows(L, inner)]
                 + [_resident(a.shape) for a in small],
        out_specs=_rows(L, inner),
        out_shape=jax.ShapeDtypeStruct((s, inner), BF16),
        scratch_shapes=[pltpu.VMEM((L + SUBLANES, width), F32),
                        pltpu.VMEM((L, width), F32),
                        pltpu.VMEM((SSD_STATE, inner), F32),
                        pltpu.VMEM((L, inner), F32)],
        compiler_params=_params(),
        name="ssd",
    )(xbc, dt, z, *small)


def _merge_kernel(yn_ref, p_ref, gt_ref, x_ref, mod_ref, wso_ref, wpg_ref, psc_ref, wpo_ref, wo_ref,
                  h_ref, pext_ref, *, tile):
    d = x_ref.shape[1]
    pw = p_ref.shape[1]
    gd = pw // len(POOL_WINDOWS)
    i = pl.program_id(0)

    @pl.when(i == 0)
    def _():
        pext_ref[0:POOL_HALO, :] = jnp.zeros((POOL_HALO, pw), F32)

    pext_ref[POOL_HALO:POOL_HALO + tile, :] = p_ref[...]
    t_glob = i * tile + lax.broadcasted_iota(jnp.int32, (tile, 1), 0) + 1
    pooled = []
    for g, w in enumerate(POOL_WINDOWS):
        gs = slice(g * gd, (g + 1) * gd)
        acc = pext_ref[:, gs]
        step = 1
        while step < w:
            acc = acc + pltpu.roll(acc, step, axis=0)
            step *= 2
        inv_cnt = 1.0 / jnp.minimum(t_glob, w).astype(F32)
        pg = acc[POOL_HALO:, :] * inv_cnt - p_ref[:, gs]
        pooled.append(_dot(pg.astype(BF16), wpg_ref[g]))
    pext_ref[0:POOL_HALO, :] = pext_ref[tile:tile + POOL_HALO, :]
    pb = (jnp.concatenate(pooled, axis=1) * psc_ref[...]).astype(BF16)
    y_b = _dot(pb, wpo_ref[...])
    y_a = _dot(yn_ref[...], wso_ref[...])
    sg = jax.nn.sigmoid(gt_ref[...])
    merged = (sg[:, :d] * y_a + sg[:, d:] * y_b).astype(BF16)
    h_ref[...] = x_ref[...] + mod_ref[2:3, :] * _dot(merged, wo_ref[...])


def _merge(yn, p, gates, x, mod, wso, wpg, psc, wpo, wo, tile):
    s, d = x.shape
    kern = functools.partial(_merge_kernel, tile=tile)
    res = (mod, wso, wpg, psc, wpo, wo)
    return pl.pallas_call(
        kern,
        grid=(s // tile,),
        in_specs=[_rows(tile, yn.shape[1]), _rows(tile, p.shape[1]), _rows(tile, gates.shape[1]),
                  _rows(tile, d)] + [_resident(a.shape) for a in res],
        out_specs=_rows(tile, d),
        out_shape=jax.ShapeDtypeStruct((s, d), F32),
        scratch_shapes=[pltpu.VMEM((tile + POOL_HALO, p.shape[1]), F32)],
        compiler_params=_params(),
        name="merge",
    )(yn, p, gates, x, *res)


def _ffn_kernel(h_ref, mod_ref, g2_ref, wv_ref, wgt_ref, cw_ref, cb_ref, wout_ref, gf_ref,
                o_ref, gpad_ref, *, tile):
    f = wv_ref.shape[1]
    tail = FFN_CONV - 1

    @pl.when(pl.program_id(0) == 0)
    def _():
        gpad_ref[0:SUBLANES, :] = jnp.zeros((SUBLANES, f), F32)

    h = h_ref[...]
    v = _modulated_rmsnorm(h, g2_ref[...], mod_ref[3:4, :], mod_ref[4:5, :]).astype(BF16)
    gpad_ref[SUBLANES:SUBLANES + tile, :] = _dot(v, wgt_ref[...])
    acc = cb_ref[...]
    for k in range(FFN_CONV):
        acc = acc + cw_ref[k:k + 1, :] * gpad_ref[pl.ds(SUBLANES - tail + k, tile), :]
    gpad_ref[0:SUBLANES, :] = gpad_ref[tile:tile + SUBLANES, :]
    act = (_silu(acc) * _dot(v, wv_ref[...])).astype(BF16)
    h2 = h + mod_ref[5:6, :] * _dot(act, wout_ref[...])
    o_ref[...] = h2 * lax.rsqrt(jnp.mean(h2 * h2, axis=-1, keepdims=True) + EPS) * gf_ref[...]


def _ffn(h, mod, g2, wv, wgt, conv_w, conv_b, wout, gf, tile):
    s, d = h.shape
    f = wv.shape[1]
    kern = functools.partial(_ffn_kernel, tile=tile)
    res = (mod, g2, wv, wgt, conv_w, conv_b, wout, gf)
    return pl.pallas_call(
        kern,
        grid=(s // tile,),
        in_specs=[_rows(tile, d)] + [_resident(a.shape) for a in res],
        out_specs=_rows(tile, d),
        out_shape=jax.ShapeDtypeStruct((s, d), F32),
        scratch_shapes=[pltpu.VMEM((tile + SUBLANES, f), F32)],
        compiler_params=_params(),
        name="ffn",
    )(h, *res)


def _layer(h, c, w_ada, b_ada, g_norm1, w_in, ssd_conv_w, ssd_conv_b, ssd_dt_bias, ssd_a_log,
           ssd_d, g_ssd_norm, w_ssd_out, w_pool_grp, pool_scale, w_pool_out, w_out, g_norm2,
           w_ffn_in, ffn_conv_w, ffn_conv_b, w_ffn_out, g_final):
    s, d = h.shape
    heads = ssd_dt_bias.shape[0]
    inner = heads * SSD_HEAD_DIM
    conv_dim = ssd_conv_w.shape[1]
    pool_w = pool_scale.shape[0]
    f = ffn_conv_b.shape[0]
    o1, o2, o3, o4 = inner, inner + conv_dim, inner + conv_dim + heads, inner + conv_dim + heads + pool_w

    mod = _ada(c, w_ada, b_ada).reshape(N_MOD, d)

    w_in_b = w_in.astype(BF16)
    wz, wx, wp, wg = w_in_b[:, :o1], w_in_b[:, o1:o2], w_in_b[:, o3:o4], w_in_b[:, o4:]
    wdt = jnp.pad(w_in_b[:, o2:o3], ((0, 0), (0, LANES - heads)))
    pad_h = lambda a: jnp.pad(a, (0, LANES - heads)).reshape(1, LANES)

    z, xbc, dt, p, gates = _front(h, mod, g_norm1.reshape(1, d), wz, wx, wdt, wp, wg, tile=256)
    yn = _ssd(xbc, dt, z, ssd_conv_w, ssd_conv_b.reshape(1, conv_dim), pad_h(ssd_dt_bias),
              pad_h(ssd_a_log), jnp.repeat(ssd_d, SSD_HEAD_DIM).reshape(1, inner),
              g_ssd_norm.reshape(1, inner), heads)
    h1 = _merge(yn, p, gates, h, mod, w_ssd_out.astype(BF16), w_pool_grp.astype(BF16),
                pool_scale.reshape(1, pool_w), w_pool_out.astype(BF16), w_out.astype(BF16), tile=256)
    w_ffn_b = w_ffn_in.astype(BF16)
    return _ffn(h1, mod, g_norm2.reshape(1, d), w_ffn_b[:, :f], w_ffn_b[:, f:], ffn_conv_w,
                ffn_conv_b.reshape(1, f), w_ffn_out.astype(BF16), g_final.reshape(1, d), tile=256)


def kernel(x, c, w_ada, b_ada, g_norm1, w_in, ssd_conv_w, ssd_conv_b, ssd_dt_bias, ssd_a_log, ssd_d,
           g_ssd_norm, w_ssd_out, w_pool_grp, pool_scale, w_pool_out, w_out, g_norm2, w_ffn_in,
           ffn_conv_w, ffn_conv_b, w_ffn_out, g_final):
    assert x.shape[0] == 1 and w_ada.shape[0] == 1, "one sequence, one layer"
    out = _layer(x[0], c, w_ada[0], b_ada[0], g_norm1[0], w_in[0], ssd_conv_w[0], ssd_conv_b[0],
                 ssd_dt_bias[0], ssd_a_log[0], ssd_d[0], g_ssd_norm[0], w_ssd_out[0], w_pool_grp[0],
                 pool_scale[0], w_pool_out[0], w_out[0], g_norm2[0], w_ffn_in[0], ffn_conv_w[0],
                 ffn_conv_b[0], w_ffn_out[0], g_final)
    return out[None]
```

```python
import functools

import jax
import jax.numpy as jnp
from jax import lax
from jax.experimental import pallas as pl
from jax.experimental.pallas import tpu as pltpu

F32 = jnp.float32
BF16 = jnp.bfloat16

EPS = 1e-6
SSD_HEAD_DIM = 64
SSD_GROUPS = 4
SSD_STATE = 128
SSD_CONV = 4
SSD_CHUNK = 128
POOL_WINDOWS = (2, 4, 8, 16)
POOL_HALO = 16
FFN_CONV = 3
N_MOD = 6
LANES = 128
SUBLANES = 8
VMEM_LIMIT = 56 * 1024 * 1024


def _dot(a, b):
    return jnp.dot(a, b, preferred_element_type=F32)


def _silu(v):
    return v * jax.nn.sigmoid(v)


def _resident(shape):
    return pl.BlockSpec(shape, lambda i: (0,) * len(shape), pipeline_mode=pl.Buffered(1))


def _rows(tile, width):
    return pl.BlockSpec((tile, width), lambda i: (i, 0))


def _params(vmem=VMEM_LIMIT):
    return pltpu.CompilerParams(dimension_semantics=("arbitrary",), vmem_limit_bytes=vmem)


def _ada_kernel(c_ref, w_ref, b_ref, o_ref):
    cond = _silu(c_ref[...])
    cond8 = jnp.broadcast_to(cond, (SUBLANES, cond.shape[1])).astype(BF16)
    o_ref[...] = _dot(cond8, w_ref[...].astype(BF16))[0:1] + b_ref[...]


def _ada(c, w_ada, b_ada):
    d, n = w_ada.shape
    blk = d
    return pl.pallas_call(
        _ada_kernel,
        grid=(n // blk,),
        in_specs=[pl.BlockSpec((1, d), lambda j: (0, 0)),
                  pl.BlockSpec((d, blk), lambda j: (0, j)),
                  pl.BlockSpec((1, blk), lambda j: (0, j))],
        out_specs=pl.BlockSpec((1, blk), lambda j: (0, j)),
        out_shape=jax.ShapeDtypeStruct((1, n), F32),
        compiler_params=_params(),
        name="ada",
    )(c, w_ada, b_ada.reshape(1, n))


def _modulated_rmsnorm(x, g, shift, scale):
    y = x * lax.rsqrt(jnp.mean(x * x, axis=-1, keepdims=True) + EPS) * g
    return y * (1.0 + scale) + shift


def _split3(v):
    hi = v.astype(BF16)
    r = v - hi.astype(F32)
    mid = r.astype(BF16)
    lo = (r - mid.astype(F32)).astype(BF16)
    return hi, mid, lo


def _causal_conv_silu(xpad_ref, xc_ref, cw_ref, cb_ref, rows):
    width = xc_ref.shape[1]
    tail = SSD_CONV - 1
    cblk = 4 * LANES
    for c0 in range(0, width, cblk):
        cs = slice(c0, c0 + cblk)
        acc = cb_ref[:, cs]
        for k in range(SSD_CONV):
            acc = acc + cw_ref[k:k + 1, cs] * xpad_ref[pl.ds(SUBLANES - tail + k, rows), cs]
        xc_ref[:, cs] = _silu(acc)
    xpad_ref[0:SUBLANES, :] = xpad_ref[rows:rows + SUBLANES, :]


def _ssd_chunk(xc_ref, r0, dtv, alog_ref, dsk_ref, st_ref, y_ref, *, inner, heads):
    L = SSD_CHUNK
    N = SSD_STATE
    P = SSD_HEAD_DIM
    hpg = heads // SSD_GROUPS
    rs = slice(r0, r0 + L)

    da = dtv * (-jnp.exp(alog_ref[...]))
    row = lax.broadcasted_iota(jnp.int32, (L, L), 0)
    col = lax.broadcasted_iota(jnp.int32, (L, L), 1)
    causal = row >= col
    tril = jnp.where(causal, 1.0, 0.0).astype(BF16)
    hi, mid, lo = _split3(da)
    acum = _dot(tril, hi) + _dot(tril, mid) + _dot(tril, lo)
    acum_t = acum.T
    dt_t = dtv.T
    last = acum[L - 1:L, :]
    wgt = dtv * jnp.exp(last - acum)
    ea_last = jnp.exp(last)

    lane = lax.broadcasted_iota(jnp.int32, (L, 2 * LANES), 1)
    lane1 = lax.broadcasted_iota(jnp.int32, (L, LANES), 1)
    lane_row = lax.broadcasted_iota(jnp.int32, (1, LANES), 1)

    for g in range(SSD_GROUPS):
        b_g = xc_ref[rs, inner + g * N: inner + (g + 1) * N]
        c_g = xc_ref[rs, inner + (SSD_GROUPS + g) * N: inner + (SSD_GROUPS + g + 1) * N]
        cb = lax.dot_general(c_g.astype(BF16), b_g.astype(BF16),
                             (((1,), (1,)), ((), ())), preferred_element_type=F32)
        scale_rows = []
        for quad in range(hpg // 4):
            h0 = g * hpg + quad * 4
            q0 = h0 * P
            lhs = []
            for h in range(h0, h0 + 4):
                colb = jnp.broadcast_to(acum[:, h:h + 1], (L, L))
                seg = colb - acum_t[h:h + 1, :]
                dec = jnp.exp(jnp.where(causal, seg, -jnp.inf))
                w_h = cb * dec * dt_t[h:h + 1, :]
                e_h = c_g * jnp.exp(colb)
                lhs.append(jnp.concatenate([w_h.astype(BF16), e_h.astype(BF16)], axis=1))
                scale_rows.append(jnp.broadcast_to(ea_last[:, h:h + 1], (1, LANES)))
            lhs = jnp.concatenate(lhs, axis=0)
            xq = xc_ref[rs, q0:q0 + 4 * P]
            rhs = jnp.concatenate([xq.astype(BF16), st_ref[:, q0:q0 + 4 * P].astype(BF16)], axis=0)
            out = _dot(lhs, rhs)
            yq = jnp.where(lane < P, out[0:L],
                           jnp.where(lane < 2 * P, out[L:2 * L],
                                     jnp.where(lane < 3 * P, out[2 * L:3 * L], out[3 * L:4 * L])))
            y_ref[rs, q0:q0 + 4 * P] = yq + dsk_ref[:, q0:q0 + 4 * P] * xq
        xw = []
        for pair in range(hpg // 2):
            h0 = g * hpg + pair * 2
            p0 = h0 * P
            wb = jnp.where(lane1 < P,
                           jnp.broadcast_to(wgt[:, h0:h0 + 1], (L, LANES)),
                           jnp.broadcast_to(wgt[:, h0 + 1:h0 + 2], (L, LANES)))
            xw.append((xc_ref[rs, p0:p0 + 2 * P] * wb).astype(BF16))
            srow = jnp.where(lane_row < P, scale_rows[pair * 2], scale_rows[pair * 2 + 1])
            st_ref[:, p0:p0 + 2 * P] = st_ref[:, p0:p0 + 2 * P] * srow
        xw = jnp.concatenate(xw, axis=1)
        g0 = g * hpg * P
        st_ref[:, g0:g0 + hpg * P] += _dot(b_g.T.astype(BF16), xw)


def _pool_mix(pext_ref, p, wpg_ref, psc_ref, tile, step_idx):
    pw = pext_ref.shape[1]
    gd = pw // len(POOL_WINDOWS)
    pext_ref[POOL_HALO:POOL_HALO + tile, :] = p
    t_glob = step_idx * tile + lax.broadcasted_iota(jnp.int32, (tile, 1), 0) + 1
    pooled = []
    for g, w in enumerate(POOL_WINDOWS):
        gs = slice(g * gd, (g + 1) * gd)
        acc = pext_ref[:, gs]
        step = 1
        while step < w:
            acc = acc + pltpu.roll(acc, step, axis=0)
            step *= 2
        inv_cnt = 1.0 / jnp.minimum(t_glob, w).astype(F32)
        pg = acc[POOL_HALO:, :] * inv_cnt - pext_ref[POOL_HALO:POOL_HALO + tile, gs]
        pooled.append(_dot(pg.astype(BF16), wpg_ref[g]))
    pext_ref[0:POOL_HALO, :] = pext_ref[tile:tile + POOL_HALO, :]
    return (jnp.concatenate(pooled, axis=1) * psc_ref[...]).astype(BF16)


def _mixer_kernel(x_ref, mod_ref, g1_ref, wz_ref, wx_ref, wdt_ref, wp_ref, wg_ref,
                  cw_ref, cb_ref, dtb_ref, alog_ref, dsk_ref, gn_ref,
                  wso_ref, wpg_ref, psc_ref, wpo_ref, wo_ref,
                  h_ref, xpad_ref, xc_ref, st_ref, y_ref, pext_ref, *, tile, heads):
    d = x_ref.shape[1]
    inner = y_ref.shape[1]
    L = SSD_CHUNK
    i = pl.program_id(0)

    @pl.when(i == 0)
    def _():
        xpad_ref[0:SUBLANES, :] = jnp.zeros((SUBLANES, xpad_ref.shape[1]), F32)
        st_ref[...] = jnp.zeros(st_ref.shape, F32)
        pext_ref[0:POOL_HALO, :] = jnp.zeros((POOL_HALO, pext_ref.shape[1]), F32)

    x = x_ref[...]
    u = _modulated_rmsnorm(x, g1_ref[...], mod_ref[0:1, :], mod_ref[1:2, :]).astype(BF16)

    xpad_ref[SUBLANES:SUBLANES + tile, :] = _dot(u, wx_ref[...])
    _causal_conv_silu(xpad_ref, xc_ref, cw_ref, cb_ref, tile)
    dtv = jax.nn.softplus(_dot(u, wdt_ref[...]) + dtb_ref[...])
    for c in range(tile // L):
        _ssd_chunk(xc_ref, c * L, dtv[c * L:(c + 1) * L], alog_ref, dsk_ref, st_ref, y_ref,
                   inner=inner, heads=heads)
    gw = inner // SSD_GROUPS
    yn = []
    for g in range(SSD_GROUPS):
        gs = slice(g * gw, (g + 1) * gw)
        t = y_ref[:, gs] * _silu(_dot(u, wz_ref[:, gs]))
        ms = jnp.mean(t * t, axis=-1, keepdims=True)
        yn.append((t * lax.rsqrt(ms + EPS) * gn_ref[:, gs]).astype(BF16))
    y_a = _dot(jnp.concatenate(yn, axis=1), wso_ref[...])

    pb = _pool_mix(pext_ref, _dot(u, wp_ref[...]), wpg_ref, psc_ref, tile, i)
    y_b = _dot(pb, wpo_ref[...])

    sg = jax.nn.sigmoid(_dot(u, wg_ref[...]))
    merged = (sg[:, :d] * y_a + sg[:, d:] * y_b).astype(BF16)
    h_ref[...] = x + mod_ref[2:3, :] * _dot(merged, wo_ref[...])


def _mixer(x, mod, g1, wz, wx, wdt, wp, wg, conv_w, conv_b, dt_bias, a_log, d_skip, g_norm,
           wso, wpg, psc, wpo, wo, heads, tile):
    s, d = x.shape
    inner = wz.shape[1]
    width = wx.shape[1]
    pw = wp.shape[1]
    kern = functools.partial(_mixer_kernel, tile=tile, heads=heads)
    res = (mod, g1, wz, wx, wdt, wp, wg, conv_w, conv_b, dt_bias, a_log, d_skip, g_norm,
           wso, wpg, psc, wpo, wo)
    return pl.pallas_call(
        kern,
        grid=(s // tile,),
        in_specs=[_rows(tile, d)] + [_resident(a.shape) for a in res],
        out_specs=_rows(tile, d),
        out_shape=jax.ShapeDtypeStruct((s, d), F32),
        scratch_shapes=[pltpu.VMEM((tile + SUBLANES, width), F32),
                        pltpu.VMEM((tile, width), F32),
                        pltpu.VMEM((SSD_STATE, inner), F32),
                        pltpu.VMEM((tile, inner), F32),
                        pltpu.VMEM((tile + POOL_HALO, pw), F32)],
        compiler_params=_params(),
        name="mixer",
    )(x, *res)


def _ffn_kernel(h_ref, mod_ref, g2_ref, wv_ref, wgt_ref, cw_ref, cb_ref, wout_ref, gf_ref,
                o_ref, gpad_ref, *, tile):
    f = wv_ref.shape[1]
    tail = FFN_CONV - 1

    @pl.when(pl.program_id(0) == 0)
    def _():
        gpad_ref[0:SUBLANES, :] = jnp.zeros((SUBLANES, f), F32)

    h = h_ref[...]
    v = _modulated_rmsnorm(h, g2_ref[...], mod_ref[3:4, :], mod_ref[4:5, :]).astype(BF16)
    gpad_ref[SUBLANES:SUBLANES + tile, :] = _dot(v, wgt_ref[...])
    acc = cb_ref[...]
    for k in range(FFN_CONV):
        acc = acc + cw_ref[k:k + 1, :] * gpad_ref[pl.ds(SUBLANES - tail + k, tile), :]
    gpad_ref[0:SUBLANES, :] = gpad_ref[tile:tile + SUBLANES, :]
    act = (_silu(acc) * _dot(v, wv_ref[...])).astype(BF16)
    h2 = h + mod_ref[5:6, :] * _dot(act, wout_ref[...])
    o_ref[...] = h2 * lax.rsqrt(jnp.mean(h2 * h2, axis=-1, keepdims=True) + EPS) * gf_ref[...]


def _ffn(h, mod, g2, wv, wgt, conv_w, conv_b, wout, gf, tile):
    s, d = h.shape
    f = wv.shape[1]
    kern = functools.partial(_ffn_kernel, tile=tile)
    res = (mod, g2, wv, wgt, conv_w, conv_b, wout, gf)
    return pl.pallas_call(
        kern,
        grid=(s // tile,),
        in_specs=[_rows(tile, d)] + [_resident(a.shape) for a in res],
        out_specs=_rows(tile, d),
        out_shape=jax.ShapeDtypeStruct((s, d), F32),
        scratch_shapes=[pltpu.VMEM((tile + SUBLANES, f), F32)],
        compiler_params=_params(),
        name="ffn",
    )(h, *res)


def _layer(h, c, w_ada, b_ada, g_norm1, w_in, ssd_conv_w, ssd_conv_b, ssd_dt_bias, ssd_a_log,
           ssd_d, g_ssd_norm, w_ssd_out, w_pool_grp, pool_scale, w_pool_out, w_out, g_norm2,
           w_ffn_in, ffn_conv_w, ffn_conv_b, w_ffn_out, g_final):
    s, d = h.shape
    heads = ssd_dt_bias.shape[0]
    inner = heads * SSD_HEAD_DIM
    conv_dim = ssd_conv_w.shape[1]
    pool_w = pool_scale.shape[0]
    f = ffn_conv_b.shape[0]
    o1, o2, o3, o4 = inner, inner + conv_dim, inner + conv_dim + heads, inner + conv_dim + heads + pool_w

    mod = _ada(c, w_ada, b_ada).reshape(N_MOD, d)

    w_in_b = w_in.astype(BF16)
    wz, wx, wp, wg = w_in_b[:, :o1], w_in_b[:, o1:o2], w_in_b[:, o3:o4], w_in_b[:, o4:]
    wdt = jnp.pad(w_in_b[:, o2:o3], ((0, 0), (0, LANES - heads)))
    pad_h = lambda a: jnp.pad(a, (0, LANES - heads)).reshape(1, LANES)

    h1 = _mixer(h, mod, g_norm1.reshape(1, d), wz, wx, wdt, wp, wg,
                ssd_conv_w, ssd_conv_b.reshape(1, conv_dim), pad_h(ssd_dt_bias), pad_h(ssd_a_log),
                jnp.repeat(ssd_d, SSD_HEAD_DIM).reshape(1, inner), g_ssd_norm.reshape(1, inner),
                w_ssd_out.astype(BF16), w_pool_grp.astype(BF16), pool_scale.reshape(1, pool_w),
                w_pool_out.astype(BF16), w_out.astype(BF16), heads, tile=256)
    w_ffn_b = w_ffn_in.astype(BF16)
    return _ffn(h1, mod, g_norm2.reshape(1, d), w_ffn_b[:, :f], w_ffn_b[:, f:], ffn_conv_w,
                ffn_conv_b.reshape(1, f), w_ffn_out.astype(BF16), g_final.reshape(1, d), tile=256)


def kernel(x, c, w_ada, b_ada, g_norm1, w_in, ssd_conv_w, ssd_conv_b, ssd_dt_bias, ssd_a_log, ssd_d,
           g_ssd_norm, w_ssd_out, w_pool_grp, pool_scale, w_pool_out, w_out, g_norm2, w_ffn_in,
           ffn_conv_w, ffn_conv_b, w_ffn_out, g_final):
    assert x.shape[0] == 1 and w_ada.shape[0] == 1, "one sequence, one layer"
    out = _layer(x[0], c, w_ada[0], b_ada[0], g_norm1[0], w_in[0], ssd_conv_w[0], ssd_conv_b[0],
                 ssd_dt_bias[0], ssd_a_log[0], ssd_d[0], g_ssd_norm[0], w_ssd_out[0], w_pool_grp[0],
                 pool_scale[0], w_pool_out[0], w_out[0], g_norm2[0], w_ffn_in[0], ffn_conv_w[0],
                 ffn_conv_b[0], w_ffn_out[0], g_final)
    return out[None]
```

```python
import functools

import jax
import jax.numpy as jnp
from jax import lax
from jax.experimental import pallas as pl
from jax.experimental.pallas import tpu as pltpu

F32 = jnp.float32
BF16 = jnp.bfloat16

EPS = 1e-6
SSD_HEAD_DIM = 64
SSD_GROUPS = 4
SSD_STATE = 128
SSD_CONV = 4
SSD_CHUNK = 128
POOL_WINDOWS = (2, 4, 8, 16)
POOL_HALO = 16
FFN_CONV = 3
N_MOD = 6
LANES = 128
SUBLANES = 8
VMEM_LIMIT = 56 * 1024 * 1024


def _dot(a, b):
    return jnp.dot(a, b, preferred_element_type=F32)


def _silu(v):
    return v * jax.nn.sigmoid(v)


def _resident(shape):
    return pl.BlockSpec(shape, lambda i: (0,) * len(shape), pipeline_mode=pl.Buffered(1))


def _rows(tile, width):
    return pl.BlockSpec((tile, width), lambda i: (i, 0))


def _params(vmem=VMEM_LIMIT):
    return pltpu.CompilerParams(dimension_semantics=("arbitrary",), vmem_limit_bytes=vmem)


def _ada_kernel(c_ref, w_ref, b_ref, o_ref):
    cond = _silu(c_ref[...])
    cond8 = jnp.broadcast_to(cond, (SUBLANES, cond.shape[1])).astype(BF16)
    o_ref[...] = _dot(cond8, w_ref[...].astype(BF16))[0:1] + b_ref[...]


def _ada(c, w_ada, b_ada):
    d, n = w_ada.shape
    blk = d
    return pl.pallas_call(
        _ada_kernel,
        grid=(n // blk,),
        in_specs=[pl.BlockSpec((1, d), lambda j: (0, 0)),
                  pl.BlockSpec((d, blk), lambda j: (0, j)),
                  pl.BlockSpec((1, blk), lambda j: (0, j))],
        out_specs=pl.BlockSpec((1, blk), lambda j: (0, j)),
        out_shape=jax.ShapeDtypeStruct((1, n), F32),
        compiler_params=_params(),
        name="ada",
    )(c, w_ada, b_ada.reshape(1, n))


def _modulated_rmsnorm(x, g, shift, scale):
    y = x * lax.rsqrt(jnp.mean(x * x, axis=-1, keepdims=True) + EPS) * g
    return y * (1.0 + scale) + shift


def _split3(v):
    hi = v.astype(BF16)
    r = v - hi.astype(F32)
    mid = r.astype(BF16)
    lo = (r - mid.astype(F32)).astype(BF16)
    return hi, mid, lo


def _conv_silu_block(xpad_ref, xc_ref, cw_ref, cb_ref, rows, cs):
    tail = SSD_CONV - 1
    acc = cb_ref[:, cs]
    for k in range(SSD_CONV):
        acc = acc + cw_ref[k:k + 1, cs] * xpad_ref[pl.ds(SUBLANES - tail + k, rows), cs]
    xc_ref[:, cs] = _silu(acc)
    xpad_ref[0:SUBLANES, cs] = xpad_ref[rows:rows + SUBLANES, cs]


def _ssd_chunk(xc_ref, r0, dtv, alog_ref, dsk_ref, st_ref, y_ref, fillers, *, inner, heads):
    L = SSD_CHUNK
    N = SSD_STATE
    P = SSD_HEAD_DIM
    hpg = heads // SSD_GROUPS
    rs = slice(r0, r0 + L)

    da = dtv * (-jnp.exp(alog_ref[...]))
    row = lax.broadcasted_iota(jnp.int32, (L, L), 0)
    col = lax.broadcasted_iota(jnp.int32, (L, L), 1)
    causal = row >= col
    tril = jnp.where(causal, 1.0, 0.0).astype(BF16)
    hi, mid, lo = _split3(da)
    acum = _dot(tril, hi) + _dot(tril, mid) + _dot(tril, lo)
    acum_t = acum.T
    dt_t = dtv.T
    last = acum[L - 1:L, :]
    wgt = dtv * jnp.exp(last - acum)
    ea_last = jnp.exp(last)

    lane = lax.broadcasted_iota(jnp.int32, (L, 2 * LANES), 1)
    lane1 = lax.broadcasted_iota(jnp.int32, (L, LANES), 1)
    lane_row = lax.broadcasted_iota(jnp.int32, (1, LANES), 1)

    for g in range(SSD_GROUPS):
        b_g = xc_ref[rs, inner + g * N: inner + (g + 1) * N]
        c_g = xc_ref[rs, inner + (SSD_GROUPS + g) * N: inner + (SSD_GROUPS + g + 1) * N]
        cb = lax.dot_general(c_g.astype(BF16), b_g.astype(BF16),
                             (((1,), (1,)), ((), ())), preferred_element_type=F32)
        scale_rows = []
        for quad in range(hpg // 4):
            h0 = g * hpg + quad * 4
            q0 = h0 * P
            lhs = []
            for h in range(h0, h0 + 4):
                colb = jnp.broadcast_to(acum[:, h:h + 1], (L, L))
                seg = colb - acum_t[h:h + 1, :]
                dec = jnp.exp(jnp.where(causal, seg, -jnp.inf))
                w_h = cb * dec * dt_t[h:h + 1, :]
                e_h = c_g * jnp.exp(colb)
                lhs.append(jnp.concatenate([w_h.astype(BF16), e_h.astype(BF16)], axis=1))
                scale_rows.append(jnp.broadcast_to(ea_last[:, h:h + 1], (1, LANES)))
            lhs = jnp.concatenate(lhs, axis=0)
            xq = xc_ref[rs, q0:q0 + 4 * P]
            rhs = jnp.concatenate([xq.astype(BF16), st_ref[:, q0:q0 + 4 * P].astype(BF16)], axis=0)
            out = _dot(lhs, rhs)
            yq = jnp.where(lane < P, out[0:L],
                           jnp.where(lane < 2 * P, out[L:2 * L],
                                     jnp.where(lane < 3 * P, out[2 * L:3 * L], out[3 * L:4 * L])))
            y_ref[rs, q0:q0 + 4 * P] = yq + dsk_ref[:, q0:q0 + 4 * P] * xq
            next(fillers, lambda: None)()
        xw = []
        for pair in range(hpg // 2):
            h0 = g * hpg + pair * 2
            p0 = h0 * P
            wb = jnp.where(lane1 < P,
                           jnp.broadcast_to(wgt[:, h0:h0 + 1], (L, LANES)),
                           jnp.broadcast_to(wgt[:, h0 + 1:h0 + 2], (L, LANES)))
            xw.append((xc_ref[rs, p0:p0 + 2 * P] * wb).astype(BF16))
            srow = jnp.where(lane_row < P, scale_rows[pair * 2], scale_rows[pair * 2 + 1])
            st_ref[:, p0:p0 + 2 * P] = st_ref[:, p0:p0 + 2 * P] * srow
        xw = jnp.concatenate(xw, axis=1)
        g0 = g * hpg * P
        st_ref[:, g0:g0 + hpg * P] += _dot(b_g.T.astype(BF16), xw)


def _pool_mix(pext_ref, wpg_ref, psc_ref, tile, step_idx):
    pw = pext_ref.shape[1]
    gd = pw // len(POOL_WINDOWS)
    t_glob = step_idx * tile + lax.broadcasted_iota(jnp.int32, (tile, 1), 0) + 1
    pooled = []
    for g, w in enumerate(POOL_WINDOWS):
        gs = slice(g * gd, (g + 1) * gd)
        acc = pext_ref[:, gs]
        step = 1
        while step < w:
            acc = acc + pltpu.roll(acc, step, axis=0)
            step *= 2
        inv_cnt = 1.0 / jnp.minimum(t_glob, w).astype(F32)
        pg = acc[POOL_HALO:, :] * inv_cnt - pext_ref[POOL_HALO:POOL_HALO + tile, gs]
        pooled.append(_dot(pg.astype(BF16), wpg_ref[g]))
    pext_ref[0:POOL_HALO, :] = pext_ref[tile:tile + POOL_HALO, :]
    return (jnp.concatenate(pooled, axis=1) * psc_ref[...]).astype(BF16)


def _mixer_kernel(x_ref, mod_ref, g1_ref, wz_ref, wx_ref, wdt_ref, wp_ref, wg_ref,
                  cw_ref, cb_ref, dtb_ref, alog_ref, dsk_ref, gn_ref,
                  wso_ref, wpg_ref, psc_ref, wpo_ref, wo_ref,
                  h_ref, xpad_ref, xc_ref, st_ref, y_ref, pext_ref, z_ref, gt_ref, *, tile, heads):
    d = x_ref.shape[1]
    inner = y_ref.shape[1]
    L = SSD_CHUNK
    i = pl.program_id(0)

    @pl.when(i == 0)
    def _():
        xpad_ref[0:SUBLANES, :] = jnp.zeros((SUBLANES, xpad_ref.shape[1]), F32)
        st_ref[...] = jnp.zeros(st_ref.shape, F32)
        pext_ref[0:POOL_HALO, :] = jnp.zeros((POOL_HALO, pext_ref.shape[1]), F32)

    x = x_ref[...]
    u = _modulated_rmsnorm(x, g1_ref[...], mod_ref[0:1, :], mod_ref[1:2, :]).astype(BF16)

    blk = 4 * LANES

    blocks = [slice(c0, c0 + blk) for c0 in range(0, xc_ref.shape[1], blk)]
    for j, cs in enumerate(blocks):
        xpad_ref[SUBLANES:SUBLANES + tile, cs] = _dot(u, wx_ref[:, cs])
        if j > 0:
            _conv_silu_block(xpad_ref, xc_ref, cw_ref, cb_ref, tile, blocks[j - 1])
    dtv = jax.nn.softplus(_dot(u, wdt_ref[...]) + dtb_ref[...])
    _conv_silu_block(xpad_ref, xc_ref, cw_ref, cb_ref, tile, blocks[-1])

    def proj(dst_ref, r_off, w_ref, cs):
        def run():
            dst_ref[r_off:r_off + tile, cs] = _dot(u, w_ref[:, cs])
        return run
    fillers = iter([proj(z_ref, 0, wz_ref, slice(c0, c0 + blk)) for c0 in range(0, inner, blk)]
                   + [proj(pext_ref, POOL_HALO, wp_ref, slice(c0, c0 + blk))
                      for c0 in range(0, pext_ref.shape[1], blk)]
                   + [proj(gt_ref, 0, wg_ref, slice(c0, c0 + blk)) for c0 in range(0, 2 * d, blk)])
    for c in range(tile // L):
        _ssd_chunk(xc_ref, c * L, dtv[c * L:(c + 1) * L], alog_ref, dsk_ref, st_ref, y_ref, fillers,
                   inner=inner, heads=heads)
    for run in fillers:
        run()

    gw = inner // SSD_GROUPS
    yn = []
    for g in range(SSD_GROUPS):
        gs = slice(g * gw, (g + 1) * gw)
        t = y_ref[:, gs] * _silu(z_ref[:, gs])
        ms = jnp.mean(t * t, axis=-1, keepdims=True)
        yn.append((t * lax.rsqrt(ms + EPS) * gn_ref[:, gs]).astype(BF16))
    y_a = _dot(jnp.concatenate(yn, axis=1), wso_ref[...])

    pb = _pool_mix(pext_ref, wpg_ref, psc_ref, tile, i)
    y_b = _dot(pb, wpo_ref[...])

    sg = jax.nn.sigmoid(gt_ref[...])
    merged = (sg[:, :d] * y_a + sg[:, d:] * y_b).astype(BF16)
    h_ref[...] = x + mod_ref[2:3, :] * _dot(merged, wo_ref[...])


def _mixer(x, mod, g1, wz, wx, wdt, wp, wg, conv_w, conv_b, dt_bias, a_log, d_skip, g_norm,
           wso, wpg, psc, wpo, wo, heads, tile):
    s, d = x.shape
    inner = wz.shape[1]
    width = wx.shape[1]
    pw = wp.shape[1]
    kern = functools.partial(_mixer_kernel, tile=tile, heads=heads)
    res = (mod, g1, wz, wx, wdt, wp, wg, conv_w, conv_b, dt_bias, a_log, d_skip, g_norm,
           wso, wpg, psc, wpo, wo)
    return pl.pallas_call(
        kern,
        grid=(s // tile,),
        in_specs=[_rows(tile, d)] + [_resident(a.shape) for a in res],
        out_specs=_rows(tile, d),
        out_shape=jax.ShapeDtypeStruct((s, d), F32),
        scratch_shapes=[pltpu.VMEM((tile + SUBLANES, width), F32),
                        pltpu.VMEM((tile, width), F32),
                        pltpu.VMEM((SSD_STATE, inner), F32),
                        pltpu.VMEM((tile, inner), F32),
                        pltpu.VMEM((tile + POOL_HALO, pw), F32),
                        pltpu.VMEM((tile, inner), F32),
                        pltpu.VMEM((tile, 2 * d), F32)],
        compiler_params=_params(),
        name="mixer",
    )(x, *res)


def _ffn_kernel(h_ref, mod_ref, g2_ref, wv_ref, wgt_ref, cw_ref, cb_ref, wout_ref, gf_ref,
                o_ref, gpad_ref, *, tile):
    f = wv_ref.shape[1]
    tail = FFN_CONV - 1

    @pl.when(pl.program_id(0) == 0)
    def _():
        gpad_ref[0:SUBLANES, :] = jnp.zeros((SUBLANES, f), F32)

    h = h_ref[...]
    v = _modulated_rmsnorm(h, g2_ref[...], mod_ref[3:4, :], mod_ref[4:5, :]).astype(BF16)
    gpad_ref[SUBLANES:SUBLANES + tile, :] = _dot(v, wgt_ref[...])
    acc = cb_ref[...]
    for k in range(FFN_CONV):
        acc = acc + cw_ref[k:k + 1, :] * gpad_ref[pl.ds(SUBLANES - tail + k, tile), :]
    gpad_ref[0:SUBLANES, :] = gpad_ref[tile:tile + SUBLANES, :]
    act = (_silu(acc) * _dot(v, wv_ref[...])).astype(BF16)
    h2 = h + mod_ref[5:6, :] * _dot(act, wout_ref[...])
    o_ref[...] = h2 * lax.rsqrt(jnp.mean(h2 * h2, axis=-1, keepdims=True) + EPS) * gf_ref[...]


def _ffn(h, mod, g2, wv, wgt, conv_w, conv_b, wout, gf, tile):
    s, d = h.shape
    f = wv.shape[1]
    kern = functools.partial(_ffn_kernel, tile=tile)
    res = (mod, g2, wv, wgt, conv_w, conv_b, wout, gf)
    return pl.pallas_call(
        kern,
        grid=(s // tile,),
        in_specs=[_rows(tile, d)] + [_resident(a.shape) for a in res],
        out_specs=_rows(tile, d),
        out_shape=jax.ShapeDtypeStruct((s, d), F32),
        scratch_shapes=[pltpu.VMEM((tile + SUBLANES, f), F32)],
        compiler_params=_params(),
        name="ffn",
    )(h, *res)


def _layer(h, c, w_ada, b_ada, g_norm1, w_in, ssd_conv_w, ssd_conv_b, ssd_dt_bias, ssd_a_log,
           ssd_d, g_ssd_norm, w_ssd_out, w_pool_grp, pool_scale, w_pool_out, w_out, g_norm2,
           w_ffn_in, ffn_conv_w, ffn_conv_b, w_ffn_out, g_final):
    s, d = h.shape
    heads = ssd_dt_bias.shape[0]
    inner = heads * SSD_HEAD_DIM
    conv_dim = ssd_conv_w.shape[1]
    pool_w = pool_scale.shape[0]
    f = ffn_conv_b.shape[0]
    o1, o2, o3, o4 = inner, inner + conv_dim, inner + conv_dim + heads, inner + conv_dim + heads + pool_w

    mod = _ada(c, w_ada, b_ada).reshape(N_MOD, d)

    w_in_b = w_in.astype(BF16)
    wz, wx, wp, wg = w_in_b[:, :o1], w_in_b[:, o1:o2], w_in_b[:, o3:o4], w_in_b[:, o4:]
    wdt = jnp.pad(w_in_b[:, o2:o3], ((0, 0), (0, LANES - heads)))
    pad_h = lambda a: jnp.pad(a, (0, LANES - heads)).reshape(1, LANES)

    h1 = _mixer(h, mod, g_norm1.reshape(1, d), wz, wx, wdt, wp, wg,
                ssd_conv_w, ssd_conv_b.reshape(1, conv_dim), pad_h(ssd_dt_bias), pad_h(ssd_a_log),
                jnp.repeat(ssd_d, SSD_HEAD_DIM).reshape(1, inner), g_ssd_norm.reshape(1, inner),
                w_ssd_out.astype(BF16), w_pool_grp.astype(BF16), pool_scale.reshape(1, pool_w),
                w_pool_out.astype(BF16), w_out.astype(BF16), heads, tile=256)
    w_ffn_b = w_ffn_in.astype(BF16)
    return _ffn(h1, mod, g_norm2.reshape(1, d), w_ffn_b[:, :f], w_ffn_b[:, f:], ffn_conv_w,
                ffn_conv_b.reshape(1, f), w_ffn_out.astype(BF16), g_final.reshape(1, d), tile=256)


def kernel(x, c, w_ada, b_ada, g_norm1, w_in, ssd_conv_w, ssd_conv_b, ssd_dt_bias, ssd_a_log, ssd_d,
           g_ssd_norm, w_ssd_out, w_pool_grp, pool_scale, w_pool_out, w_out, g_norm2, w_ffn_in,
           ffn_conv_w, ffn_conv_b, w_ffn_out, g_final):
    assert x.shape[0] == 1 and w_ada.shape[0] == 1, "one sequence, one layer"
    out = _layer(x[0], c, w_ada[0], b_ada[0], g_norm1[0], w_in[0], ssd_conv_w[0], ssd_conv_b[0],
                 ssd_dt_bias[0], ssd_a_log[0], ssd_d[0], g_ssd_norm[0], w_ssd_out[0], w_pool_grp[0],
                 pool_scale[0], w_pool_out[0], w_out[0], g_norm2[0], w_ffn_in[0], ffn_conv_w[0],
                 ffn_conv_b[0], w_ffn_out[0], g_final)
    return out[None]
```

```python
import functools

import jax
import jax.numpy as jnp
from jax import lax
from jax.experimental import pallas as pl
from jax.experimental.pallas import tpu as pltpu

F32 = jnp.float32
BF16 = jnp.bfloat16

EPS = 1e-6
LOG2E = 1.4426950408889634
SSD_HEAD_DIM = 64
SSD_GROUPS = 4
SSD_STATE = 128
SSD_CONV = 4
SSD_CHUNK = 128
POOL_WINDOWS = (2, 4, 8, 16)
POOL_HALO = 16
FFN_CONV = 3
N_MOD = 6
LANES = 128
SUBLANES = 8
VMEM_LIMIT = 56 * 1024 * 1024
MIXER_TILE = 256
FFN_TILE = 512
WEIGHT_ROWS = 128


def _dot(a, b):
    return jnp.dot(a, b, preferred_element_type=F32)


def _silu(v):
    return v * jax.nn.sigmoid(v)


def _resident(shape):
    return pl.BlockSpec(shape, lambda i: (0,) * len(shape), pipeline_mode=pl.Buffered(1))


def _rows(tile, width):
    return pl.BlockSpec((tile, width), lambda i: (i, 0))


def _params(vmem=VMEM_LIMIT):
    return pltpu.CompilerParams(dimension_semantics=("arbitrary",), vmem_limit_bytes=vmem)


def _stream_cast(src_hbm, row_chunk, store, row0=0, rows=None):
    rows = src_hbm.shape[0] - row0 if rows is None else rows
    width = src_hbm.shape[1]
    n = rows // row_chunk
    assert n * row_chunk == rows

    def body(stage, sem):
        def copy(k, slot):
            src = src_hbm.at[pl.ds(row0 + k * row_chunk, row_chunk), :]
            return pltpu.make_async_copy(src, stage.at[slot], sem.at[slot])
        copy(0, 0).start()
        for k in range(n):
            slot = k % 2
            if k + 1 < n:
                copy(k + 1, 1 - slot).start()
            copy(k, slot).wait()
            store(k, stage[slot])

    pl.run_scoped(body, pltpu.VMEM((2, row_chunk, width), F32), pltpu.SemaphoreType.DMA((2,)))


def _cast_into(dst_ref, row_chunk):
    def store(k, chunk):
        dst_ref[k * row_chunk:(k + 1) * row_chunk, :] = chunk.astype(dst_ref.dtype)
    return store


def _cast_transposed_into(dst_ref, row_chunk, col0=0):
    def store(k, chunk):
        c = col0 + k * row_chunk
        dst_ref[:, c:c + row_chunk] = chunk.T.astype(dst_ref.dtype)
    return store


def _ada_kernel(c_ref, w_ref, b_ref, o_ref):
    cond = _silu(c_ref[...])
    cond8 = jnp.broadcast_to(cond, (SUBLANES, cond.shape[1])).astype(BF16)
    o_ref[...] = _dot(cond8, w_ref[...].astype(BF16))[0:1] + b_ref[...]


def _ada(c, w_ada, b_ada):
    d, n = w_ada.shape
    blk = d
    return pl.pallas_call(
        _ada_kernel,
        grid=(n // blk,),
        in_specs=[pl.BlockSpec((1, d), lambda j: (0, 0)),
                  pl.BlockSpec((d, blk), lambda j: (0, j)),
                  pl.BlockSpec((1, blk), lambda j: (0, j))],
        out_specs=pl.BlockSpec((1, blk), lambda j: (0, j)),
        out_shape=jax.ShapeDtypeStruct((1, n), F32),
        compiler_params=_params(),
        name="ada",
    )(c, w_ada, b_ada.reshape(1, n))


def _modulated_rmsnorm(x, g, shift, scale):
    y = x * lax.rsqrt(jnp.mean(x * x, axis=-1, keepdims=True) + EPS) * g
    return y * (1.0 + scale) + shift


def _split3(v):
    hi = v.astype(BF16)
    r = v - hi.astype(F32)
    mid = r.astype(BF16)
    lo = (r - mid.astype(F32)).astype(BF16)
    return hi, mid, lo


def _conv_silu_block(xblk, halo_ref, xc_ref, cw_ref, cb_ref, cs):
    rows = xblk.shape[0]
    xb = jnp.concatenate([halo_ref[:, cs], xblk], axis=0)
    acc = cb_ref[:, cs] + cw_ref[SSD_CONV - 1:SSD_CONV, cs] * xblk
    for shift in range(1, SSD_CONV):
        k = SSD_CONV - 1 - shift
        acc = acc + cw_ref[k:k + 1, cs] * pltpu.roll(xb, shift, axis=0)[SUBLANES:]
    xc_ref[:, cs] = _silu(acc)
    halo_ref[:, cs] = xblk[rows - SUBLANES:]


def _ssd_chunk(xc_ref, r0, dtv, alog_ref, dsk_ref, st_ref, y_ref, fillers, *, inner, heads):
    L = SSD_CHUNK
    N = SSD_STATE
    P = SSD_HEAD_DIM
    hpg = heads // SSD_GROUPS
    rs = slice(r0, r0 + L)

    da = dtv * (-jnp.exp(alog_ref[...]))
    row = lax.broadcasted_iota(jnp.int32, (L, L), 0)
    col = lax.broadcasted_iota(jnp.int32, (L, L), 1)
    causal = row >= col
    tril = jnp.where(causal, 1.0, 0.0).astype(BF16)
    hi, mid, lo = _split3(da)
    acum = (_dot(tril, hi) + _dot(tril, mid) + _dot(tril, lo)) * LOG2E
    src_t = (acum - jnp.log2(dtv)).T
    last = acum[L - 1:L, :]
    wgt = dtv * jnp.exp2(last - acum)
    ea_last = jnp.exp2(last)

    lane = lax.broadcasted_iota(jnp.int32, (L, 2 * LANES), 1)
    lane1 = lax.broadcasted_iota(jnp.int32, (L, LANES), 1)
    lane_row = lax.broadcasted_iota(jnp.int32, (1, LANES), 1)

    for g in range(SSD_GROUPS):
        b_g = xc_ref[rs, inner + g * N: inner + (g + 1) * N]
        c_g = xc_ref[rs, inner + (SSD_GROUPS + g) * N: inner + (SSD_GROUPS + g + 1) * N]
        cb = lax.dot_general(c_g.astype(BF16), b_g.astype(BF16),
                             (((1,), (1,)), ((), ())), preferred_element_type=F32)
        scale_rows = []
        for quad in range(hpg // 4):
            h0 = g * hpg + quad * 4
            q0 = h0 * P
            lhs = []
            for h in range(h0, h0 + 4):
                colb = jnp.broadcast_to(acum[:, h:h + 1], (L, L))
                w_h = cb * jnp.exp2(jnp.where(causal, colb - src_t[h:h + 1, :], -jnp.inf))
                e_h = c_g * jnp.exp2(colb)
                lhs.append(jnp.concatenate([w_h.astype(BF16), e_h.astype(BF16)], axis=1))
                scale_rows.append(jnp.broadcast_to(ea_last[:, h:h + 1], (1, LANES)))
            lhs = jnp.concatenate(lhs, axis=0)
            xq = xc_ref[rs, q0:q0 + 4 * P]
            rhs = jnp.concatenate([xq.astype(BF16), st_ref[:, q0:q0 + 4 * P].astype(BF16)], axis=0)
            out = _dot(lhs, rhs)
            yq = jnp.where(lane < P, out[0:L],
                           jnp.where(lane < 2 * P, out[L:2 * L],
                                     jnp.where(lane < 3 * P, out[2 * L:3 * L], out[3 * L:4 * L])))
            y_ref[rs, q0:q0 + 4 * P] = yq + dsk_ref[:, q0:q0 + 4 * P] * xq
            next(fillers, lambda: None)()
        xw = []
        for pair in range(hpg // 2):
            h0 = g * hpg + pair * 2
            p0 = h0 * P
            wb = jnp.where(lane1 < P,
                           jnp.broadcast_to(wgt[:, h0:h0 + 1], (L, LANES)),
                           jnp.broadcast_to(wgt[:, h0 + 1:h0 + 2], (L, LANES)))
            xw.append((xc_ref[rs, p0:p0 + 2 * P] * wb).astype(BF16))
            srow = jnp.where(lane_row < P, scale_rows[pair * 2], scale_rows[pair * 2 + 1])
            st_ref[:, p0:p0 + 2 * P] = st_ref[:, p0:p0 + 2 * P] * srow
        xw = jnp.concatenate(xw, axis=1)
        g0 = g * hpg * P
        st_ref[:, g0:g0 + hpg * P] += _dot(b_g.T.astype(BF16), xw)


def _pool_mix(pext_ref, wpg_ref, psc_ref, tile, step_idx):
    pw = pext_ref.shape[1]
    gd = pw // len(POOL_WINDOWS)
    t_glob = step_idx * tile + lax.broadcasted_iota(jnp.int32, (tile, 1), 0) + 1
    pooled = []
    for g, w in enumerate(POOL_WINDOWS):
        gs = slice(g * gd, (g + 1) * gd)
        acc = pext_ref[:, gs]
        step = 1
        while step < w:
            acc = acc + pltpu.roll(acc, step, axis=0)
            step *= 2
        inv_cnt = 1.0 / jnp.minimum(t_glob, w).astype(F32)
        pg = acc[POOL_HALO:, :] * inv_cnt - pext_ref[POOL_HALO:POOL_HALO + tile, gs]
        pooled.append(_dot(pg.astype(BF16), wpg_ref[g]))
    pext_ref[0:POOL_HALO, :] = pext_ref[tile:tile + POOL_HALO, :]
    return (jnp.concatenate(pooled, axis=1) * psc_ref[...]).astype(BF16)


def _mixer_kernel(x_ref, mod_ref, g1_ref, w_in_t_hbm,
                  cw_ref, cb_ref, dtb_ref, alog_ref, dsk_ref, gn_ref,
                  wso_hbm, wpg_ref, psc_ref, wpo_hbm, wo_hbm,
                  h_ref, halo_ref, xc_ref, st_ref, y_ref, pext_ref, z_ref, gt_ref,
                  wzx_ref, wdt_ref, wp_ref, wg_ref, wso_ref, wpo_ref, wo_ref, *, tile, heads):
    d = x_ref.shape[1]
    inner = y_ref.shape[1]
    L = SSD_CHUNK
    i = pl.program_id(0)

    @pl.when(i == 0)
    def _():
        halo_ref[...] = jnp.zeros(halo_ref.shape, F32)
        st_ref[...] = jnp.zeros(st_ref.shape, F32)
        pext_ref[0:POOL_HALO, :] = jnp.zeros((POOL_HALO, pext_ref.shape[1]), F32)
        tc = 2 * LANES
        n_zx, pw, gw2 = wzx_ref.shape[1], wp_ref.shape[1], wg_ref.shape[1]
        _stream_cast(w_in_t_hbm, tc, _cast_transposed_into(wzx_ref, tc), 0, n_zx)

        def store_dt(k, chunk):
            lane = lax.broadcasted_iota(jnp.int32, (chunk.shape[1], LANES), 1)
            wdt_ref[...] = jnp.where(lane < heads, chunk.T, 0.0).astype(BF16)
        _stream_cast(w_in_t_hbm, LANES, store_dt, n_zx, LANES)
        _stream_cast(w_in_t_hbm, tc, _cast_transposed_into(wp_ref, tc), n_zx + heads, pw)
        _stream_cast(w_in_t_hbm, tc, _cast_transposed_into(wg_ref, tc), n_zx + heads + pw, gw2)
        for src, dst in ((wso_hbm, wso_ref), (wpo_hbm, wpo_ref), (wo_hbm, wo_ref)):
            _stream_cast(src, 4 * WEIGHT_ROWS, _cast_into(dst, 4 * WEIGHT_ROWS))

    x = x_ref[...]
    u = _modulated_rmsnorm(x, g1_ref[...], mod_ref[0:1, :], mod_ref[1:2, :]).astype(BF16)

    blk = 4 * LANES

    def proj(dst_ref, r_off, w_ref, w_off, cs):
        def run():
            dst_ref[r_off:r_off + tile, cs] = _dot(u, w_ref[:, w_off + cs.start:w_off + cs.stop])
        return run

    for c0 in range(0, xc_ref.shape[1], blk):
        xblk = _dot(u, wzx_ref[:, inner + c0:inner + c0 + blk])
        _conv_silu_block(xblk, halo_ref, xc_ref, cw_ref, cb_ref, slice(c0, c0 + blk))
    dtv = jax.nn.softplus(_dot(u, wdt_ref[...]) + dtb_ref[...])

    fillers = iter([proj(z_ref, 0, wzx_ref, 0, slice(c0, c0 + blk)) for c0 in range(0, inner, blk)]
                   + [proj(pext_ref, POOL_HALO, wp_ref, 0, slice(c0, c0 + blk))
                      for c0 in range(0, pext_ref.shape[1], blk)]
                   + [proj(gt_ref, 0, wg_ref, 0, slice(c0, c0 + blk)) for c0 in range(0, 2 * d, blk)])
    for c in range(tile // L):
        _ssd_chunk(xc_ref, c * L, dtv[c * L:(c + 1) * L], alog_ref, dsk_ref, st_ref, y_ref, fillers,
                   inner=inner, heads=heads)
    for run in fillers:
        run()

    gw = inner // SSD_GROUPS
    yn = []
    for g in range(SSD_GROUPS):
        gs = slice(g * gw, (g + 1) * gw)
        t = y_ref[:, gs] * _silu(z_ref[:, gs])
        ms = jnp.mean(t * t, axis=-1, keepdims=True)
        yn.append((t * lax.rsqrt(ms + EPS) * gn_ref[:, gs]).astype(BF16))
    y_a = _dot(jnp.concatenate(yn, axis=1), wso_ref[...])

    pb = _pool_mix(pext_ref, wpg_ref, psc_ref, tile, i)
    y_b = _dot(pb, wpo_ref[...])

    sg = jax.nn.sigmoid(gt_ref[...])
    merged = (sg[:, :d] * y_a + sg[:, d:] * y_b).astype(BF16)
    h_ref[...] = x + mod_ref[2:3, :] * _dot(merged, wo_ref[...])


def _mixer(x, mod, g1, w_in_t, conv_w, conv_b, dt_bias, a_log, d_skip, g_norm,
           wso, wpg, psc, wpo, wo, heads, tile):
    s, d = x.shape
    inner = heads * SSD_HEAD_DIM
    width = conv_w.shape[1]
    pw = psc.shape[1]
    kern = functools.partial(_mixer_kernel, tile=tile, heads=heads)
    hbm = pl.BlockSpec(memory_space=pl.ANY)
    small = lambda a: _resident(a.shape)
    return pl.pallas_call(
        kern,
        grid=(s // tile,),
        in_specs=[_rows(tile, d), small(mod), small(g1), hbm,
                  small(conv_w), small(conv_b), small(dt_bias), small(a_log), small(d_skip), small(g_norm),
                  hbm, small(wpg), small(psc), hbm, hbm],
        out_specs=_rows(tile, d),
        out_shape=jax.ShapeDtypeStruct((s, d), F32),
        scratch_shapes=[pltpu.VMEM((SUBLANES, width), F32),
                        pltpu.VMEM((tile, width), F32),
                        pltpu.VMEM((SSD_STATE, inner), F32),
                        pltpu.VMEM((tile, inner), F32),
                        pltpu.VMEM((tile + POOL_HALO, pw), F32),
                        pltpu.VMEM((tile, inner), F32),
                        pltpu.VMEM((tile, 2 * d), F32),
                        pltpu.VMEM((d, inner + width), BF16),
                        pltpu.VMEM((d, LANES), BF16),
                        pltpu.VMEM((d, pw), BF16),
                        pltpu.VMEM((d, 2 * d), BF16),
                        pltpu.VMEM(wso.shape, BF16),
                        pltpu.VMEM(wpo.shape, BF16),
                        pltpu.VMEM(wo.shape, BF16)],
        compiler_params=_params(),
        name="mixer",
    )(x, mod, g1, w_in_t, conv_w, conv_b, dt_bias, a_log, d_skip, g_norm, wso, wpg, psc, wpo, wo)


def _ffn_kernel(h_ref, mod_ref, g2_ref, win_hbm, cw_ref, cb_ref, wout_hbm, gf_ref,
                o_ref, gpad_ref, act_ref, win_ref, wout_ref, *, tile):
    f = wout_ref.shape[0]
    blk = 4 * LANES

    @pl.when(pl.program_id(0) == 0)
    def _():
        gpad_ref[0:SUBLANES, :] = jnp.zeros((SUBLANES, f), F32)
        _stream_cast(win_hbm, WEIGHT_ROWS, _cast_into(win_ref, WEIGHT_ROWS))
        out_rows = f // 4
        _stream_cast(wout_hbm, out_rows, _cast_into(wout_ref, out_rows))

    h = h_ref[...]
    v = _modulated_rmsnorm(h, g2_ref[...], mod_ref[3:4, :], mod_ref[4:5, :]).astype(BF16)
    blocks = [slice(c0, min(c0 + blk, f)) for c0 in range(0, f, blk)]

    def gate_proj(cs):
        gpad_ref[SUBLANES:SUBLANES + tile, cs] = _dot(v, win_ref[:, f + cs.start:f + cs.stop])

    def conv_glu(cs):
        gb = gpad_ref[0:tile + SUBLANES, cs]
        acc = cb_ref[:, cs] + cw_ref[FFN_CONV - 1:FFN_CONV, cs] * gb[SUBLANES:]
        for shift in range(1, FFN_CONV):
            k = FFN_CONV - 1 - shift
            acc = acc + cw_ref[k:k + 1, cs] * pltpu.roll(gb, shift, axis=0)[SUBLANES:]
        gpad_ref[0:SUBLANES, cs] = gpad_ref[tile:tile + SUBLANES, cs]
        act_ref[:, cs] = (_silu(acc) * _dot(v, win_ref[:, cs])).astype(BF16)

    for j, cs in enumerate(blocks):
        gate_proj(cs)
        if j > 0:
            conv_glu(blocks[j - 1])
    conv_glu(blocks[-1])
    h2 = h + mod_ref[5:6, :] * _dot(act_ref[...], wout_ref[...])
    o_ref[...] = h2 * lax.rsqrt(jnp.mean(h2 * h2, axis=-1, keepdims=True) + EPS) * gf_ref[...]


def _ffn(h, mod, g2, win, conv_w, conv_b, wout, gf, tile):
    s, d = h.shape
    f = wout.shape[0]
    kern = functools.partial(_ffn_kernel, tile=tile)
    hbm = pl.BlockSpec(memory_space=pl.ANY)
    small = lambda a: _resident(a.shape)
    return pl.pallas_call(
        kern,
        grid=(s // tile,),
        in_specs=[_rows(tile, d), small(mod), small(g2), hbm, small(conv_w), small(conv_b), hbm, small(gf)],
        out_specs=_rows(tile, d),
        out_shape=jax.ShapeDtypeStruct((s, d), F32),
        scratch_shapes=[pltpu.VMEM((tile + SUBLANES, f), F32),
                        pltpu.VMEM((tile, f), BF16),
                        pltpu.VMEM(win.shape, BF16),
                        pltpu.VMEM(wout.shape, BF16)],
        compiler_params=_params(),
        name="ffn",
    )(h, mod, g2, win, conv_w, conv_b, wout, gf)


def _layer(h, c, w_ada, b_ada, g_norm1, w_in, ssd_conv_w, ssd_conv_b, ssd_dt_bias, ssd_a_log,
           ssd_d, g_ssd_norm, w_ssd_out, w_pool_grp, pool_scale, w_pool_out, w_out, g_norm2,
           w_ffn_in, ffn_conv_w, ffn_conv_b, w_ffn_out, g_final):
    s, d = h.shape
    heads = ssd_dt_bias.shape[0]
    inner = heads * SSD_HEAD_DIM
    conv_dim = ssd_conv_w.shape[1]
    pool_w = pool_scale.shape[0]
    f = ffn_conv_b.shape[0]
    mod = _ada(c, w_ada, b_ada).reshape(N_MOD, d)

    pad_h = lambda a: jnp.pad(a, (0, LANES - heads)).reshape(1, LANES)
    h1 = _mixer(h, mod, g_norm1.reshape(1, d), w_in.T,
                ssd_conv_w, ssd_conv_b.reshape(1, conv_dim), pad_h(ssd_dt_bias), pad_h(ssd_a_log),
                jnp.repeat(ssd_d, SSD_HEAD_DIM).reshape(1, inner), g_ssd_norm.reshape(1, inner),
                w_ssd_out, w_pool_grp.astype(BF16), pool_scale.reshape(1, pool_w), w_pool_out, w_out,
                heads, tile=MIXER_TILE)
    return _ffn(h1, mod, g_norm2.reshape(1, d), w_ffn_in, ffn_conv_w, ffn_conv_b.reshape(1, f),
                w_ffn_out, g_final.reshape(1, d), tile=FFN_TILE)


def kernel(x, c, w_ada, b_ada, g_norm1, w_in, ssd_conv_w, ssd_conv_b, ssd_dt_bias, ssd_a_log, ssd_d,
           g_ssd_norm, w_ssd_out, w_pool_grp, pool_scale, w_pool_out, w_out, g_norm2, w_ffn_in,
           ffn_conv_w, ffn_conv_b, w_ffn_out, g_final):
    assert x.shape[0] == 1 and w_ada.shape[0] == 1, "one sequence, one layer"
    out = _layer(x[0], c, w_ada[0], b_ada[0], g_norm1[0], w_in[0], ssd_conv_w[0], ssd_conv_b[0],
                 ssd_dt_bias[0], ssd_a_log[0], ssd_d[0], g_ssd_norm[0], w_ssd_out[0], w_pool_grp[0],
                 pool_scale[0], w_pool_out[0], w_out[0], g_norm2[0], w_ffn_in[0], ffn_conv_w[0],
                 ffn_conv_b[0], w_ffn_out[0], g_final)
    return out[None]
```

```python
import functools

import jax
import jax.numpy as jnp
from jax import lax
from jax.experimental import pallas as pl
from jax.experimental.pallas import tpu as pltpu

F32 = jnp.float32
BF16 = jnp.bfloat16

EPS = 1e-6
LOG2E = 1.4426950408889634
SSD_HEAD_DIM = 64
SSD_GROUPS = 4
SSD_STATE = 128
SSD_CONV = 4
SSD_CHUNK = 128
POOL_WINDOWS = (2, 4, 8, 16)
POOL_HALO = 16
FFN_CONV = 3
N_MOD = 6
LANES = 128
SUBLANES = 8
VMEM_LIMIT = 56 * 1024 * 1024
MIXER_TILE = 256
FFN_TILE = 512
WEIGHT_ROWS = 128
STAGE_SLOTS = 4


def _dot(a, b):
    return jnp.dot(a, b, preferred_element_type=F32)


def _silu(v):
    return v * jax.nn.sigmoid(v)


def _resident(shape):
    return pl.BlockSpec(shape, lambda i: (0,) * len(shape), pipeline_mode=pl.Buffered(1))


def _rows(tile, width):
    return pl.BlockSpec((tile, width), lambda i: (i, 0))


def _params(vmem=VMEM_LIMIT):
    return pltpu.CompilerParams(dimension_semantics=("arbitrary",), vmem_limit_bytes=vmem)


def _stream_cast(src_hbm, row_chunk, store, row0=0, rows=None):
    rows = src_hbm.shape[0] - row0 if rows is None else rows
    width = src_hbm.shape[1]
    n = rows // row_chunk
    assert n * row_chunk == rows

    def body(stage, sem):
        def copy(k):
            slot = k % STAGE_SLOTS
            src = src_hbm.at[pl.ds(row0 + k * row_chunk, row_chunk), :]
            return pltpu.make_async_copy(src, stage.at[slot], sem.at[slot])
        for k in range(min(STAGE_SLOTS - 1, n)):
            copy(k).start()
        for k in range(n):
            if k + STAGE_SLOTS - 1 < n:
                copy(k + STAGE_SLOTS - 1).start()
            copy(k).wait()
            store(k, stage[k % STAGE_SLOTS])

    pl.run_scoped(body, pltpu.VMEM((STAGE_SLOTS, row_chunk, width), F32),
                  pltpu.SemaphoreType.DMA((STAGE_SLOTS,)))


def _cast_into(dst_ref, row_chunk):
    def store(k, chunk):
        dst_ref[k * row_chunk:(k + 1) * row_chunk, :] = chunk.astype(dst_ref.dtype)
    return store


def _cast_transposed_into(dst_ref, row_chunk, col0=0):
    def store(k, chunk):
        c = col0 + k * row_chunk
        dst_ref[:, c:c + row_chunk] = chunk.T.astype(dst_ref.dtype)
    return store


def _ada_kernel(c_ref, w_ref, b_ref, o_ref):
    cond = _silu(c_ref[...])
    cond8 = jnp.broadcast_to(cond, (SUBLANES, cond.shape[1])).astype(BF16)
    o_ref[...] = _dot(cond8, w_ref[...].astype(BF16))[0:1] + b_ref[...]


def _ada(c, w_ada, b_ada):
    d, n = w_ada.shape
    blk = d
    return pl.pallas_call(
        _ada_kernel,
        grid=(n // blk,),
        in_specs=[pl.BlockSpec((1, d), lambda j: (0, 0)),
                  pl.BlockSpec((d, blk), lambda j: (0, j)),
                  pl.BlockSpec((1, blk), lambda j: (0, j))],
        out_specs=pl.BlockSpec((1, blk), lambda j: (0, j)),
        out_shape=jax.ShapeDtypeStruct((1, n), F32),
        compiler_params=_params(),
        name="ada",
    )(c, w_ada, b_ada.reshape(1, n))


def _modulated_rmsnorm(x, g, shift, scale):
    y = x * lax.rsqrt(jnp.mean(x * x, axis=-1, keepdims=True) + EPS) * g
    return y * (1.0 + scale) + shift


def _split3(v):
    hi = v.astype(BF16)
    r = v - hi.astype(F32)
    mid = r.astype(BF16)
    lo = (r - mid.astype(F32)).astype(BF16)
    return hi, mid, lo


def _conv_silu_block(xblk, halo_ref, xc_ref, cw_ref, cb_ref, cs):
    rows = xblk.shape[0]
    xb = jnp.concatenate([halo_ref[:, cs], xblk], axis=0)
    acc = cb_ref[:, cs] + cw_ref[SSD_CONV - 1:SSD_CONV, cs] * xblk
    for shift in range(1, SSD_CONV):
        k = SSD_CONV - 1 - shift
        acc = acc + cw_ref[k:k + 1, cs] * pltpu.roll(xb, shift, axis=0)[SUBLANES:]
    xc_ref[:, cs] = _silu(acc)
    halo_ref[:, cs] = xblk[rows - SUBLANES:]


def _ssd_chunk(xc_ref, r0, dtv, alog_ref, dsk_ref, st_ref, y_ref, fillers, *, inner, heads):
    L = SSD_CHUNK
    N = SSD_STATE
    P = SSD_HEAD_DIM
    hpg = heads // SSD_GROUPS
    rs = slice(r0, r0 + L)

    da = dtv * (-jnp.exp(alog_ref[...]))
    row = lax.broadcasted_iota(jnp.int32, (L, L), 0)
    col = lax.broadcasted_iota(jnp.int32, (L, L), 1)
    causal = row >= col
    tril = jnp.where(causal, 1.0, 0.0).astype(BF16)
    hi, mid, lo = _split3(da)
    acum = (_dot(tril, hi) + _dot(tril, mid) + _dot(tril, lo)) * LOG2E
    src_t = (acum - jnp.log2(dtv)).T
    last = acum[L - 1:L, :]
    wgt = dtv * jnp.exp2(last - acum)
    ea_last = jnp.exp2(last)

    lane = lax.broadcasted_iota(jnp.int32, (L, 2 * LANES), 1)
    lane1 = lax.broadcasted_iota(jnp.int32, (L, LANES), 1)
    lane_row = lax.broadcasted_iota(jnp.int32, (1, LANES), 1)

    for g in range(SSD_GROUPS):
        b_g = xc_ref[rs, inner + g * N: inner + (g + 1) * N]
        c_g = xc_ref[rs, inner + (SSD_GROUPS + g) * N: inner + (SSD_GROUPS + g + 1) * N]
        cb = lax.dot_general(c_g.astype(BF16), b_g.astype(BF16),
                             (((1,), (1,)), ((), ())), preferred_element_type=F32)
        scale_rows = []
        for quad in range(hpg // 4):
            h0 = g * hpg + quad * 4
            q0 = h0 * P
            lhs = []
            for h in range(h0, h0 + 4):
                colb = jnp.broadcast_to(acum[:, h:h + 1], (L, L))
                w_h = cb * jnp.exp2(jnp.where(causal, colb - src_t[h:h + 1, :], -jnp.inf))
                e_h = c_g * jnp.exp2(colb)
                lhs.append(jnp.concatenate([w_h.astype(BF16), e_h.astype(BF16)], axis=1))
                scale_rows.append(jnp.broadcast_to(ea_last[:, h:h + 1], (1, LANES)))
            lhs = jnp.concatenate(lhs, axis=0)
            xq = xc_ref[rs, q0:q0 + 4 * P]
            rhs = jnp.concatenate([xq.astype(BF16), st_ref[:, q0:q0 + 4 * P].astype(BF16)], axis=0)
            out = _dot(lhs, rhs)
            yq = jnp.where(lane < P, out[0:L],
                           jnp.where(lane < 2 * P, out[L:2 * L],
                                     jnp.where(lane < 3 * P, out[2 * L:3 * L], out[3 * L:4 * L])))
            y_ref[rs, q0:q0 + 4 * P] = yq + dsk_ref[:, q0:q0 + 4 * P] * xq
            next(fillers, lambda: None)()
        xw = []
        for pair in range(hpg // 2):
            h0 = g * hpg + pair * 2
            p0 = h0 * P
            wb = jnp.where(lane1 < P,
                           jnp.broadcast_to(wgt[:, h0:h0 + 1], (L, LANES)),
                           jnp.broadcast_to(wgt[:, h0 + 1:h0 + 2], (L, LANES)))
            xw.append((xc_ref[rs, p0:p0 + 2 * P] * wb).astype(BF16))
            srow = jnp.where(lane_row < P, scale_rows[pair * 2], scale_rows[pair * 2 + 1])
            st_ref[:, p0:p0 + 2 * P] = st_ref[:, p0:p0 + 2 * P] * srow
        xw = jnp.concatenate(xw, axis=1)
        g0 = g * hpg * P
        st_ref[:, g0:g0 + hpg * P] += _dot(b_g.T.astype(BF16), xw)


def _pool_mix(pext_ref, wpg_ref, psc_ref, tile, step_idx):
    pw = pext_ref.shape[1]
    gd = pw // len(POOL_WINDOWS)
    t_glob = step_idx * tile + lax.broadcasted_iota(jnp.int32, (tile, 1), 0) + 1
    pooled = []
    for g, w in enumerate(POOL_WINDOWS):
        gs = slice(g * gd, (g + 1) * gd)
        acc = pext_ref[:, gs]
        step = 1
        while step < w:
            acc = acc + pltpu.roll(acc, step, axis=0)
            step *= 2
        inv_cnt = 1.0 / jnp.minimum(t_glob, w).astype(F32)
        pg = acc[POOL_HALO:, :] * inv_cnt - pext_ref[POOL_HALO:POOL_HALO + tile, gs]
        pooled.append(_dot(pg.astype(BF16), wpg_ref[g]))
    pext_ref[0:POOL_HALO, :] = pext_ref[tile:tile + POOL_HALO, :]
    return (jnp.concatenate(pooled, axis=1) * psc_ref[...]).astype(BF16)


def _mixer_kernel(x_ref, mod_ref, g1_ref, w_in_t_hbm,
                  cw_ref, cb_ref, dtb_ref, alog_ref, dsk_ref, gn_ref,
                  wso_hbm, wpg_ref, psc_ref, wpo_hbm, wo_hbm,
                  h_ref, halo_ref, xc_ref, st_ref, y_ref, pext_ref, z_ref, gt_ref,
                  wzx_ref, wdt_ref, wp_ref, wg_ref, wso_ref, wpo_ref, wo_ref, *, tile, heads):
    d = x_ref.shape[1]
    inner = y_ref.shape[1]
    L = SSD_CHUNK
    i = pl.program_id(0)

    @pl.when(i == 0)
    def _():
        halo_ref[...] = jnp.zeros(halo_ref.shape, F32)
        st_ref[...] = jnp.zeros(st_ref.shape, F32)
        pext_ref[0:POOL_HALO, :] = jnp.zeros((POOL_HALO, pext_ref.shape[1]), F32)
        tc = 2 * LANES
        n_zx, pw, gw2 = wzx_ref.shape[1], wp_ref.shape[1], wg_ref.shape[1]
        _stream_cast(w_in_t_hbm, tc, _cast_transposed_into(wzx_ref, tc), 0, n_zx)

        def store_dt(k, chunk):
            lane = lax.broadcasted_iota(jnp.int32, (chunk.shape[1], LANES), 1)
            wdt_ref[...] = jnp.where(lane < heads, chunk.T, 0.0).astype(BF16)
        _stream_cast(w_in_t_hbm, LANES, store_dt, n_zx, LANES)
        _stream_cast(w_in_t_hbm, tc, _cast_transposed_into(wp_ref, tc), n_zx + heads, pw)
        _stream_cast(w_in_t_hbm, tc, _cast_transposed_into(wg_ref, tc), n_zx + heads + pw, gw2)
        for src, dst in ((wso_hbm, wso_ref), (wpo_hbm, wpo_ref), (wo_hbm, wo_ref)):
            _stream_cast(src, 2 * WEIGHT_ROWS, _cast_into(dst, 2 * WEIGHT_ROWS))

    x = x_ref[...]
    u = _modulated_rmsnorm(x, g1_ref[...], mod_ref[0:1, :], mod_ref[1:2, :]).astype(BF16)

    blk = 4 * LANES

    def proj(dst_ref, r_off, w_ref, w_off, cs):
        def run():
            dst_ref[r_off:r_off + tile, cs] = _dot(u, w_ref[:, w_off + cs.start:w_off + cs.stop])
        return run

    for c0 in range(0, xc_ref.shape[1], blk):
        xblk = _dot(u, wzx_ref[:, inner + c0:inner + c0 + blk])
        _conv_silu_block(xblk, halo_ref, xc_ref, cw_ref, cb_ref, slice(c0, c0 + blk))
    dtv = jax.nn.softplus(_dot(u, wdt_ref[...]) + dtb_ref[...])

    fillers = iter([proj(z_ref, 0, wzx_ref, 0, slice(c0, c0 + blk)) for c0 in range(0, inner, blk)]
                   + [proj(pext_ref, POOL_HALO, wp_ref, 0, slice(c0, c0 + blk))
                      for c0 in range(0, pext_ref.shape[1], blk)]
                   + [proj(gt_ref, 0, wg_ref, 0, slice(c0, c0 + blk)) for c0 in range(0, 2 * d, blk)])
    for c in range(tile // L):
        _ssd_chunk(xc_ref, c * L, dtv[c * L:(c + 1) * L], alog_ref, dsk_ref, st_ref, y_ref, fillers,
                   inner=inner, heads=heads)
    for run in fillers:
        run()

    gw = inner // SSD_GROUPS
    yn = []
    for g in range(SSD_GROUPS):
        gs = slice(g * gw, (g + 1) * gw)
        t = y_ref[:, gs] * _silu(z_ref[:, gs])
        ms = jnp.mean(t * t, axis=-1, keepdims=True)
        yn.append((t * lax.rsqrt(ms + EPS) * gn_ref[:, gs]).astype(BF16))
    y_a = _dot(jnp.concatenate(yn, axis=1), wso_ref[...])

    pb = _pool_mix(pext_ref, wpg_ref, psc_ref, tile, i)
    y_b = _dot(pb, wpo_ref[...])

    sg = jax.nn.sigmoid(gt_ref[...])
    merged = (sg[:, :d] * y_a + sg[:, d:] * y_b).astype(BF16)
    h_ref[...] = x + mod_ref[2:3, :] * _dot(merged, wo_ref[...])


def _mixer(x, mod, g1, w_in_t, conv_w, conv_b, dt_bias, a_log, d_skip, g_norm,
           wso, wpg, psc, wpo, wo, heads, tile):
    s, d = x.shape
    inner = heads * SSD_HEAD_DIM
    width = conv_w.shape[1]
    pw = psc.shape[1]
    kern = functools.partial(_mixer_kernel, tile=tile, heads=heads)
    hbm = pl.BlockSpec(memory_space=pl.ANY)
    small = lambda a: _resident(a.shape)
    return pl.pallas_call(
        kern,
        grid=(s // tile,),
        in_specs=[_rows(tile, d), small(mod), small(g1), hbm,
                  small(conv_w), small(conv_b), small(dt_bias), small(a_log), small(d_skip), small(g_norm),
                  hbm, small(wpg), small(psc), hbm, hbm],
        out_specs=_rows(tile, d),
        out_shape=jax.ShapeDtypeStruct((s, d), F32),
        scratch_shapes=[pltpu.VMEM((SUBLANES, width), F32),
                        pltpu.VMEM((tile, width), F32),
                        pltpu.VMEM((SSD_STATE, inner), F32),
                        pltpu.VMEM((tile, inner), F32),
                        pltpu.VMEM((tile + POOL_HALO, pw), F32),
                        pltpu.VMEM((tile, inner), F32),
                        pltpu.VMEM((tile, 2 * d), F32),
                        pltpu.VMEM((d, inner + width), BF16),
                        pltpu.VMEM((d, LANES), BF16),
                        pltpu.VMEM((d, pw), BF16),
                        pltpu.VMEM((d, 2 * d), BF16),
                        pltpu.VMEM(wso.shape, BF16),
                        pltpu.VMEM(wpo.shape, BF16),
                        pltpu.VMEM(wo.shape, BF16)],
        compiler_params=_params(),
        name="mixer",
    )(x, mod, g1, w_in_t, conv_w, conv_b, dt_bias, a_log, d_skip, g_norm, wso, wpg, psc, wpo, wo)


def _ffn_kernel(h_ref, mod_ref, g2_ref, win_hbm, cw_ref, cb_ref, wout_hbm, gf_ref,
                o_ref, gpad_ref, act_ref, win_ref, wout_ref, *, tile):
    f = wout_ref.shape[0]
    blk = 4 * LANES

    @pl.when(pl.program_id(0) == 0)
    def _():
        gpad_ref[0:SUBLANES, :] = jnp.zeros((SUBLANES, f), F32)
        in_rows = WEIGHT_ROWS // 2
        _stream_cast(win_hbm, in_rows, _cast_into(win_ref, in_rows))
        out_rows = f // 8
        _stream_cast(wout_hbm, out_rows, _cast_into(wout_ref, out_rows))

    h = h_ref[...]
    v = _modulated_rmsnorm(h, g2_ref[...], mod_ref[3:4, :], mod_ref[4:5, :]).astype(BF16)
    blocks = [slice(c0, min(c0 + blk, f)) for c0 in range(0, f, blk)]

    def gate_proj(cs):
        gpad_ref[SUBLANES:SUBLANES + tile, cs] = _dot(v, win_ref[:, f + cs.start:f + cs.stop])

    def conv_glu(cs):
        gb = gpad_ref[0:tile + SUBLANES, cs]
        acc = cb_ref[:, cs] + cw_ref[FFN_CONV - 1:FFN_CONV, cs] * gb[SUBLANES:]
        for shift in range(1, FFN_CONV):
            k = FFN_CONV - 1 - shift
            acc = acc + cw_ref[k:k + 1, cs] * pltpu.roll(gb, shift, axis=0)[SUBLANES:]
        gpad_ref[0:SUBLANES, cs] = gpad_ref[tile:tile + SUBLANES, cs]
        act_ref[:, cs] = (_silu(acc) * _dot(v, win_ref[:, cs])).astype(BF16)

    for j, cs in enumerate(blocks):
        gate_proj(cs)
        if j > 0:
            conv_glu(blocks[j - 1])
    conv_glu(blocks[-1])
    h2 = h + mod_ref[5:6, :] * _dot(act_ref[...], wout_ref[...])
    o_ref[...] = h2 * lax.rsqrt(jnp.mean(h2 * h2, axis=-1, keepdims=True) + EPS) * gf_ref[...]


def _ffn(h, mod, g2, win, conv_w, conv_b, wout, gf, tile):
    s, d = h.shape
    f = wout.shape[0]
    kern = functools.partial(_ffn_kernel, tile=tile)
    hbm = pl.BlockSpec(memory_space=pl.ANY)
    small = lambda a: _resident(a.shape)
    return pl.pallas_call(
        kern,
        grid=(s // tile,),
        in_specs=[_rows(tile, d), small(mod), small(g2), hbm, small(conv_w), small(conv_b), hbm, small(gf)],
        out_specs=_rows(tile, d),
        out_shape=jax.ShapeDtypeStruct((s, d), F32),
        scratch_shapes=[pltpu.VMEM((tile + SUBLANES, f), F32),
                        pltpu.VMEM((tile, f), BF16),
                        pltpu.VMEM(win.shape, BF16),
                        pltpu.VMEM(wout.shape, BF16)],
        compiler_params=_params(),
        name="ffn",
    )(h, mod, g2, win, conv_w, conv_b, wout, gf)


def _layer(h, c, w_ada, b_ada, g_norm1, w_in, ssd_conv_w, ssd_conv_b, ssd_dt_bias, ssd_a_log,
           ssd_d, g_ssd_norm, w_ssd_out, w_pool_grp, pool_scale, w_pool_out, w_out, g_norm2,
           w_ffn_in, ffn_conv_w, ffn_conv_b, w_ffn_out, g_final):
    s, d = h.shape
    heads = ssd_dt_bias.shape[0]
    inner = heads * SSD_HEAD_DIM
    conv_dim = ssd_conv_w.shape[1]
    pool_w = pool_scale.shape[0]
    f = ffn_conv_b.shape[0]
    mod = _ada(c, w_ada, b_ada).reshape(N_MOD, d)

    pad_h = lambda a: jnp.pad(a, (0, LANES - heads)).reshape(1, LANES)
    h1 = _mixer(h, mod, g_norm1.reshape(1, d), w_in.T,
                ssd_conv_w, ssd_conv_b.reshape(1, conv_dim), pad_h(ssd_dt_bias), pad_h(ssd_a_log),
                jnp.repeat(ssd_d, SSD_HEAD_DIM).reshape(1, inner), g_ssd_norm.reshape(1, inner),
                w_ssd_out, w_pool_grp.astype(BF16), pool_scale.reshape(1, pool_w), w_pool_out, w_out,
                heads, tile=MIXER_TILE)
    return _ffn(h1, mod, g_norm2.reshape(1, d), w_ffn_in, ffn_conv_w, ffn_conv_b.reshape(1, f),
                w_ffn_out, g_final.reshape(1, d), tile=FFN_TILE)


def kernel(x, c, w_ada, b_ada, g_norm1, w_in, ssd_conv_w, ssd_conv_b, ssd_dt_bias, ssd_a_log, ssd_d,
           g_ssd_norm, w_ssd_out, w_pool_grp, pool_scale, w_pool_out, w_out, g_norm2, w_ffn_in,
           ffn_conv_w, ffn_conv_b, w_ffn_out, g_final):
    assert x.shape[0] == 1 and w_ada.shape[0] == 1, "one sequence, one layer"
    out = _layer(x[0], c, w_ada[0], b_ada[0], g_norm1[0], w_in[0], ssd_conv_w[0], ssd_conv_b[0],
                 ssd_dt_bias[0], ssd_a_log[0], ssd_d[0], g_ssd_norm[0], w_ssd_out[0], w_pool_grp[0],
                 pool_scale[0], w_pool_out[0], w_out[0], g_norm2[0], w_ffn_in[0], ffn_conv_w[0],
                 ffn_conv_b[0], w_ffn_out[0], g_final)
    return out[None]
```

```python
import functools

import jax
import jax.numpy as jnp
from jax import lax
from jax.experimental import pallas as pl
from jax.experimental.pallas import tpu as pltpu

F32 = jnp.float32
BF16 = jnp.bfloat16

EPS = 1e-6
LOG2E = 1.4426950408889634
SSD_HEAD_DIM = 64
SSD_GROUPS = 4
SSD_STATE = 128
SSD_CONV = 4
SSD_CHUNK = 128
POOL_WINDOWS = (2, 4, 8, 16)
POOL_HALO = 16
FFN_CONV = 3
N_MOD = 6
LANES = 128
SUBLANES = 8
VMEM_LIMIT = 56 * 1024 * 1024
MIXER_TILE = 256
FFN_TILE = 512
WEIGHT_ROWS = 128
STAGE_SLOTS = 4


def _dot(a, b):
    return jnp.dot(a, b, preferred_element_type=F32)


def _dot_nt(a, b):
    return lax.dot_general(a, b, (((1,), (1,)), ((), ())), preferred_element_type=F32)


def _silu(v):
    return v * jax.nn.sigmoid(v)


def _resident(shape):
    return pl.BlockSpec(shape, lambda i: (0,) * len(shape), pipeline_mode=pl.Buffered(1))


def _rows(tile, width):
    return pl.BlockSpec((tile, width), lambda i: (i, 0))


def _params(vmem=VMEM_LIMIT):
    return pltpu.CompilerParams(dimension_semantics=("arbitrary",), vmem_limit_bytes=vmem)


def _stream_cast(src_hbm, row_chunk, store, row0=0, rows=None):
    rows = src_hbm.shape[0] - row0 if rows is None else rows
    width = src_hbm.shape[1]
    n = rows // row_chunk
    assert n * row_chunk == rows

    def body(stage, sem):
        def copy(k):
            slot = k % STAGE_SLOTS
            src = src_hbm.at[pl.ds(row0 + k * row_chunk, row_chunk), :]
            return pltpu.make_async_copy(src, stage.at[slot], sem.at[slot])
        for k in range(min(STAGE_SLOTS - 1, n)):
            copy(k).start()
        for k in range(n):
            if k + STAGE_SLOTS - 1 < n:
                copy(k + STAGE_SLOTS - 1).start()
            copy(k).wait()
            store(k, stage[k % STAGE_SLOTS])

    pl.run_scoped(body, pltpu.VMEM((STAGE_SLOTS, row_chunk, width), F32),
                  pltpu.SemaphoreType.DMA((STAGE_SLOTS,)))


def _cast_into(dst_ref, row_chunk):
    def store(k, chunk):
        dst_ref[k * row_chunk:(k + 1) * row_chunk, :] = chunk.astype(dst_ref.dtype)
    return store


def _cast_transposed_into(dst_ref, row_chunk, col0=0):
    def store(k, chunk):
        c = col0 + k * row_chunk
        dst_ref[:, c:c + row_chunk] = chunk.T.astype(dst_ref.dtype)
    return store


def _ada_kernel(c_ref, w_ref, b_ref, o_ref):
    cond = _silu(c_ref[...])
    cond8 = jnp.broadcast_to(cond, (SUBLANES, cond.shape[1])).astype(BF16)
    o_ref[...] = _dot(cond8, w_ref[...].astype(BF16))[0:1] + b_ref[...]


def _ada(c, w_ada, b_ada):
    d, n = w_ada.shape
    blk = d
    return pl.pallas_call(
        _ada_kernel,
        grid=(n // blk,),
        in_specs=[pl.BlockSpec((1, d), lambda j: (0, 0)),
                  pl.BlockSpec((d, blk), lambda j: (0, j)),
                  pl.BlockSpec((1, blk), lambda j: (0, j))],
        out_specs=pl.BlockSpec((1, blk), lambda j: (0, j)),
        out_shape=jax.ShapeDtypeStruct((1, n), F32),
        compiler_params=_params(),
        name="ada",
    )(c, w_ada, b_ada.reshape(1, n))


def _modulated_rmsnorm(x, g, shift, scale):
    y = x * lax.rsqrt(jnp.mean(x * x, axis=-1, keepdims=True) + EPS) * g
    return y * (1.0 + scale) + shift


def _split3(v):
    hi = v.astype(BF16)
    r = v - hi.astype(F32)
    mid = r.astype(BF16)
    lo = (r - mid.astype(F32)).astype(BF16)
    return hi, mid, lo


def _conv_silu_block(xblk, halo_ref, xc_ref, cw_ref, cb_ref, cs):
    rows = xblk.shape[0]
    xb = jnp.concatenate([halo_ref[:, cs], xblk], axis=0)
    acc = cb_ref[:, cs] + cw_ref[SSD_CONV - 1:SSD_CONV, cs] * xblk
    for shift in range(1, SSD_CONV):
        k = SSD_CONV - 1 - shift
        acc = acc + cw_ref[k:k + 1, cs] * pltpu.roll(xb, shift, axis=0)[SUBLANES:]
    xc_ref[:, cs] = _silu(acc)
    halo_ref[:, cs] = xblk[rows - SUBLANES:]


def _ssd_chunk(xc_ref, r0, dtv, alog_ref, dsk_ref, st_ref, y_ref, fillers, *, inner, heads):
    L = SSD_CHUNK
    N = SSD_STATE
    P = SSD_HEAD_DIM
    hpg = heads // SSD_GROUPS
    rs = slice(r0, r0 + L)

    da = dtv * (-jnp.exp(alog_ref[...]))
    row = lax.broadcasted_iota(jnp.int32, (L, L), 0)
    col = lax.broadcasted_iota(jnp.int32, (L, L), 1)
    causal = row >= col
    tril = jnp.where(causal, 1.0, 0.0).astype(BF16)
    hi, mid, lo = _split3(da)
    acum = (_dot(tril, hi) + _dot(tril, mid) + _dot(tril, lo)) * LOG2E
    src_t = (acum - jnp.log2(dtv)).T
    last = acum[L - 1:L, :]
    wgt = dtv * jnp.exp2(last - acum)
    ea_last = jnp.exp2(last)

    lane = lax.broadcasted_iota(jnp.int32, (L, 2 * LANES), 1)
    lane1 = lax.broadcasted_iota(jnp.int32, (L, LANES), 1)
    lane_row = lax.broadcasted_iota(jnp.int32, (1, LANES), 1)

    for g in range(SSD_GROUPS):
        b_g = xc_ref[rs, inner + g * N: inner + (g + 1) * N]
        c_g = xc_ref[rs, inner + (SSD_GROUPS + g) * N: inner + (SSD_GROUPS + g + 1) * N]
        cb = lax.dot_general(c_g.astype(BF16), b_g.astype(BF16),
                             (((1,), (1,)), ((), ())), preferred_element_type=F32)
        scale_rows = []
        for quad in range(hpg // 4):
            h0 = g * hpg + quad * 4
            q0 = h0 * P
            lhs = []
            for h in range(h0, h0 + 4):
                colb = jnp.broadcast_to(acum[:, h:h + 1], (L, L))
                w_h = cb * jnp.exp2(jnp.where(causal, colb - src_t[h:h + 1, :], -jnp.inf))
                e_h = c_g * jnp.exp2(colb)
                lhs.append(jnp.concatenate([w_h.astype(BF16), e_h.astype(BF16)], axis=1))
                scale_rows.append(jnp.broadcast_to(ea_last[:, h:h + 1], (1, LANES)))
            lhs = jnp.concatenate(lhs, axis=0)
            xq = xc_ref[rs, q0:q0 + 4 * P]
            rhs = jnp.concatenate([xq.astype(BF16), st_ref[:, q0:q0 + 4 * P].astype(BF16)], axis=0)
            out = _dot(lhs, rhs)
            yq = jnp.where(lane < P, out[0:L],
                           jnp.where(lane < 2 * P, out[L:2 * L],
                                     jnp.where(lane < 3 * P, out[2 * L:3 * L], out[3 * L:4 * L])))
            y_ref[rs, q0:q0 + 4 * P] = yq + dsk_ref[:, q0:q0 + 4 * P] * xq
            next(fillers, lambda: None)()
        xw = []
        for pair in range(hpg // 2):
            h0 = g * hpg + pair * 2
            p0 = h0 * P
            wb = jnp.where(lane1 < P,
                           jnp.broadcast_to(wgt[:, h0:h0 + 1], (L, LANES)),
                           jnp.broadcast_to(wgt[:, h0 + 1:h0 + 2], (L, LANES)))
            xw.append((xc_ref[rs, p0:p0 + 2 * P] * wb).astype(BF16))
            srow = jnp.where(lane_row < P, scale_rows[pair * 2], scale_rows[pair * 2 + 1])
            st_ref[:, p0:p0 + 2 * P] = st_ref[:, p0:p0 + 2 * P] * srow
        xw = jnp.concatenate(xw, axis=1)
        g0 = g * hpg * P
        st_ref[:, g0:g0 + hpg * P] += _dot(b_g.T.astype(BF16), xw)


def _pool_mix(pext_ref, wpg_ref, psc_ref, tile, step_idx):
    pw = pext_ref.shape[1]
    gd = pw // len(POOL_WINDOWS)
    t_glob = step_idx * tile + lax.broadcasted_iota(jnp.int32, (tile, 1), 0) + 1
    pooled = []
    for g, w in enumerate(POOL_WINDOWS):
        gs = slice(g * gd, (g + 1) * gd)
        acc = pext_ref[:, gs]
        step = 1
        while step < w:
            acc = acc + pltpu.roll(acc, step, axis=0)
            step *= 2
        inv_cnt = 1.0 / jnp.minimum(t_glob, w).astype(F32)
        pg = acc[POOL_HALO:, :] * inv_cnt - pext_ref[POOL_HALO:POOL_HALO + tile, gs]
        pooled.append(_dot(pg.astype(BF16), wpg_ref[g]))
    pext_ref[0:POOL_HALO, :] = pext_ref[tile:tile + POOL_HALO, :]
    return (jnp.concatenate(pooled, axis=1) * psc_ref[...]).astype(BF16)


def _mixer_kernel(x_ref, mod_ref, g1_ref, w_in_t_hbm,
                  cw_ref, cb_ref, dtb_ref, alog_ref, dsk_ref, gn_ref,
                  wso_hbm, wpg_ref, psc_ref, wpo_hbm, wo_hbm,
                  h_ref, halo_ref, xc_ref, st_ref, y_ref, pext_ref, z_ref, gt_ref,
                  wzx_ref, wdt_ref, wp_ref, wg_ref, wso_ref, wpo_ref, wo_ref, *, tile, heads):
    d = x_ref.shape[1]
    inner = y_ref.shape[1]
    L = SSD_CHUNK
    i = pl.program_id(0)

    @pl.when(i == 0)
    def _():
        halo_ref[...] = jnp.zeros(halo_ref.shape, F32)
        st_ref[...] = jnp.zeros(st_ref.shape, F32)
        pext_ref[0:POOL_HALO, :] = jnp.zeros((POOL_HALO, pext_ref.shape[1]), F32)
        tc = 2 * LANES
        n_zx, pw, gw2 = wzx_ref.shape[0], wp_ref.shape[0], wg_ref.shape[0]
        _stream_cast(w_in_t_hbm, tc, _cast_into(wzx_ref, tc), 0, n_zx)

        def store_dt(k, chunk):
            lane = lax.broadcasted_iota(jnp.int32, (chunk.shape[1], LANES), 1)
            wdt_ref[...] = jnp.where(lane < heads, chunk.T, 0.0).astype(BF16)
        _stream_cast(w_in_t_hbm, LANES, store_dt, n_zx, LANES)
        _stream_cast(w_in_t_hbm, tc, _cast_into(wp_ref, tc), n_zx + heads, pw)
        _stream_cast(w_in_t_hbm, tc, _cast_into(wg_ref, tc), n_zx + heads + pw, gw2)
        for src, dst in ((wso_hbm, wso_ref), (wpo_hbm, wpo_ref), (wo_hbm, wo_ref)):
            _stream_cast(src, 2 * WEIGHT_ROWS, _cast_into(dst, 2 * WEIGHT_ROWS))

    x = x_ref[...]
    u = _modulated_rmsnorm(x, g1_ref[...], mod_ref[0:1, :], mod_ref[1:2, :]).astype(BF16)

    blk = 4 * LANES

    def proj(dst_ref, r_off, w_ref, w_off, cs):
        def run():
            dst_ref[r_off:r_off + tile, cs] = _dot_nt(u, w_ref[w_off + cs.start:w_off + cs.stop, :])
        return run

    for c0 in range(0, xc_ref.shape[1], blk):
        xblk = _dot_nt(u, wzx_ref[inner + c0:inner + c0 + blk, :])
        _conv_silu_block(xblk, halo_ref, xc_ref, cw_ref, cb_ref, slice(c0, c0 + blk))
    dtv = jax.nn.softplus(_dot(u, wdt_ref[...]) + dtb_ref[...])

    fillers = iter([proj(z_ref, 0, wzx_ref, 0, slice(c0, c0 + blk)) for c0 in range(0, inner, blk)]
                   + [proj(pext_ref, POOL_HALO, wp_ref, 0, slice(c0, c0 + blk))
                      for c0 in range(0, pext_ref.shape[1], blk)]
                   + [proj(gt_ref, 0, wg_ref, 0, slice(c0, c0 + blk)) for c0 in range(0, 2 * d, blk)])
    for c in range(tile // L):
        _ssd_chunk(xc_ref, c * L, dtv[c * L:(c + 1) * L], alog_ref, dsk_ref, st_ref, y_ref, fillers,
                   inner=inner, heads=heads)
    for run in fillers:
        run()

    gw = inner // SSD_GROUPS
    yn = []
    for g in range(SSD_GROUPS):
        gs = slice(g * gw, (g + 1) * gw)
        t = y_ref[:, gs] * _silu(z_ref[:, gs])
        ms = jnp.mean(t * t, axis=-1, keepdims=True)
        yn.append((t * lax.rsqrt(ms + EPS) * gn_ref[:, gs]).astype(BF16))
    y_a = _dot(jnp.concatenate(yn, axis=1), wso_ref[...])

    pb = _pool_mix(pext_ref, wpg_ref, psc_ref, tile, i)
    y_b = _dot(pb, wpo_ref[...])

    sg = jax.nn.sigmoid(gt_ref[...])
    merged = (sg[:, :d] * y_a + sg[:, d:] * y_b).astype(BF16)
    h_ref[...] = x + mod_ref[2:3, :] * _dot(merged, wo_ref[...])


def _mixer(x, mod, g1, w_in_t, conv_w, conv_b, dt_bias, a_log, d_skip, g_norm,
           wso, wpg, psc, wpo, wo, heads, tile):
    s, d = x.shape
    inner = heads * SSD_HEAD_DIM
    width = conv_w.shape[1]
    pw = psc.shape[1]
    kern = functools.partial(_mixer_kernel, tile=tile, heads=heads)
    hbm = pl.BlockSpec(memory_space=pl.ANY)
    small = lambda a: _resident(a.shape)
    return pl.pallas_call(
        kern,
        grid=(s // tile,),
        in_specs=[_rows(tile, d), small(mod), small(g1), hbm,
                  small(conv_w), small(conv_b), small(dt_bias), small(a_log), small(d_skip), small(g_norm),
                  hbm, small(wpg), small(psc), hbm, hbm],
        out_specs=_rows(tile, d),
        out_shape=jax.ShapeDtypeStruct((s, d), F32),
        scratch_shapes=[pltpu.VMEM((SUBLANES, width), F32),
                        pltpu.VMEM((tile, width), F32),
                        pltpu.VMEM((SSD_STATE, inner), F32),
                        pltpu.VMEM((tile, inner), F32),
                        pltpu.VMEM((tile + POOL_HALO, pw), F32),
                        pltpu.VMEM((tile, inner), F32),
                        pltpu.VMEM((tile, 2 * d), F32),
                        pltpu.VMEM((inner + width, d), BF16),
                        pltpu.VMEM((d, LANES), BF16),
                        pltpu.VMEM((pw, d), BF16),
                        pltpu.VMEM((2 * d, d), BF16),
                        pltpu.VMEM(wso.shape, BF16),
                        pltpu.VMEM(wpo.shape, BF16),
                        pltpu.VMEM(wo.shape, BF16)],
        compiler_params=_params(),
        name="mixer",
    )(x, mod, g1, w_in_t, conv_w, conv_b, dt_bias, a_log, d_skip, g_norm, wso, wpg, psc, wpo, wo)


def _ffn_kernel(h_ref, mod_ref, g2_ref, win_hbm, cw_ref, cb_ref, wout_hbm, gf_ref,
                o_ref, gpad_ref, act_ref, win_ref, wout_ref, *, tile):
    f = wout_ref.shape[0]
    blk = 4 * LANES

    @pl.when(pl.program_id(0) == 0)
    def _():
        gpad_ref[0:SUBLANES, :] = jnp.zeros((SUBLANES, f), F32)
        in_rows = WEIGHT_ROWS // 2
        _stream_cast(win_hbm, in_rows, _cast_into(win_ref, in_rows))
        out_rows = f // 8
        _stream_cast(wout_hbm, out_rows, _cast_into(wout_ref, out_rows))

    h = h_ref[...]
    v = _modulated_rmsnorm(h, g2_ref[...], mod_ref[3:4, :], mod_ref[4:5, :]).astype(BF16)
    blocks = [slice(c0, min(c0 + blk, f)) for c0 in range(0, f, blk)]

    def gate_proj(cs):
        gpad_ref[SUBLANES:SUBLANES + tile, cs] = _dot(v, win_ref[:, f + cs.start:f + cs.stop])

    def conv_glu(cs):
        gb = gpad_ref[0:tile + SUBLANES, cs]
        acc = cb_ref[:, cs] + cw_ref[FFN_CONV - 1:FFN_CONV, cs] * gb[SUBLANES:]
        for shift in range(1, FFN_CONV):
            k = FFN_CONV - 1 - shift
            acc = acc + cw_ref[k:k + 1, cs] * pltpu.roll(gb, shift, axis=0)[SUBLANES:]
        gpad_ref[0:SUBLANES, cs] = gpad_ref[tile:tile + SUBLANES, cs]
        act_ref[:, cs] = (_silu(acc) * _dot(v, win_ref[:, cs])).astype(BF16)

    for j, cs in enumerate(blocks):
        gate_proj(cs)
        if j > 0:
            conv_glu(blocks[j - 1])
    conv_glu(blocks[-1])
    h2 = h + mod_ref[5:6, :] * _dot(act_ref[...], wout_ref[...])
    o_ref[...] = h2 * lax.rsqrt(jnp.mean(h2 * h2, axis=-1, keepdims=True) + EPS) * gf_ref[...]


def _ffn(h, mod, g2, win, conv_w, conv_b, wout, gf, tile):
    s, d = h.shape
    f = wout.shape[0]
    kern = functools.partial(_ffn_kernel, tile=tile)
    hbm = pl.BlockSpec(memory_space=pl.ANY)
    small = lambda a: _resident(a.shape)
    return pl.pallas_call(
        kern,
        grid=(s // tile,),
        in_specs=[_rows(tile, d), small(mod), small(g2), hbm, small(conv_w), small(conv_b), hbm, small(gf)],
        out_specs=_rows(tile, d),
        out_shape=jax.ShapeDtypeStruct((s, d), F32),
        scratch_shapes=[pltpu.VMEM((tile + SUBLANES, f), F32),
                        pltpu.VMEM((tile, f), BF16),
                        pltpu.VMEM(win.shape, BF16),
                        pltpu.VMEM(wout.shape, BF16)],
        compiler_params=_params(),
        name="ffn",
    )(h, mod, g2, win, conv_w, conv_b, wout, gf)


def _layer(h, c, w_ada, b_ada, g_norm1, w_in, ssd_conv_w, ssd_conv_b, ssd_dt_bias, ssd_a_log,
           ssd_d, g_ssd_norm, w_ssd_out, w_pool_grp, pool_scale, w_pool_out, w_out, g_norm2,
           w_ffn_in, ffn_conv_w, ffn_conv_b, w_ffn_out, g_final):
    s, d = h.shape
    heads = ssd_dt_bias.shape[0]
    inner = heads * SSD_HEAD_DIM
    conv_dim = ssd_conv_w.shape[1]
    pool_w = pool_scale.shape[0]
    f = ffn_conv_b.shape[0]
    mod = _ada(c, w_ada, b_ada).reshape(N_MOD, d)

    pad_h = lambda a: jnp.pad(a, (0, LANES - heads)).reshape(1, LANES)
    h1 = _mixer(h, mod, g_norm1.reshape(1, d), w_in.T,
                ssd_conv_w, ssd_conv_b.reshape(1, conv_dim), pad_h(ssd_dt_bias), pad_h(ssd_a_log),
                jnp.repeat(ssd_d, SSD_HEAD_DIM).reshape(1, inner), g_ssd_norm.reshape(1, inner),
                w_ssd_out, w_pool_grp.astype(BF16), pool_scale.reshape(1, pool_w), w_pool_out, w_out,
                heads, tile=MIXER_TILE)
    return _ffn(h1, mod, g_norm2.reshape(1, d), w_ffn_in, ffn_conv_w, ffn_conv_b.reshape(1, f),
                w_ffn_out, g_final.reshape(1, d), tile=FFN_TILE)


def kernel(x, c, w_ada, b_ada, g_norm1, w_in, ssd_conv_w, ssd_conv_b, ssd_dt_bias, ssd_a_log, ssd_d,
           g_ssd_norm, w_ssd_out, w_pool_grp, pool_scale, w_pool_out, w_out, g_norm2, w_ffn_in,
           ffn_conv_w, ffn_conv_b, w_ffn_out, g_final):
    assert x.shape[0] == 1 and w_ada.shape[0] == 1, "one sequence, one layer"
    out = _layer(x[0], c, w_ada[0], b_ada[0], g_norm1[0], w_in[0], ssd_conv_w[0], ssd_conv_b[0],
                 ssd_dt_bias[0], ssd_a_log[0], ssd_d[0], g_ssd_norm[0], w_ssd_out[0], w_pool_grp[0],
                 pool_scale[0], w_pool_out[0], w_out[0], g_norm2[0], w_ffn_in[0], ffn_conv_w[0],
                 ffn_conv_b[0], w_ffn_out[0], g_final)
    return out[None]
```

```python
import functools

import jax
import jax.numpy as jnp
from jax import lax
from jax.experimental import pallas as pl
from jax.experimental.pallas import tpu as pltpu

F32 = jnp.float32
BF16 = jnp.bfloat16

EPS = 1e-6
LOG2E = 1.4426950408889634
SSD_HEAD_DIM = 64
SSD_GROUPS = 4
SSD_STATE = 128
SSD_CONV = 4
SSD_CHUNK = 128
POOL_WINDOWS = (2, 4, 8, 16)
POOL_HALO = 16
FFN_CONV = 3
N_MOD = 6
LANES = 128
SUBLANES = 8
VMEM_LIMIT = 56 * 1024 * 1024
MIXER_TILE = 256
FFN_TILE = 512
MIXER_COLS = 4 * LANES
FFN_COLS = 2 * LANES
WEIGHT_ROWS = 128
STAGE_SLOTS = 4


def _dot(a, b):
    return jnp.dot(a, b, preferred_element_type=F32)


def _dot_nt(a, b):
    return lax.dot_general(a, b, (((1,), (1,)), ((), ())), preferred_element_type=F32)


def _silu(v):
    return v * jax.nn.sigmoid(v)


def _resident(shape):
    return pl.BlockSpec(shape, lambda i: (0,) * len(shape), pipeline_mode=pl.Buffered(1))


def _rows(tile, width):
    return pl.BlockSpec((tile, width), lambda i: (i, 0))


def _params(vmem=VMEM_LIMIT):
    return pltpu.CompilerParams(dimension_semantics=("arbitrary",), vmem_limit_bytes=vmem)


def _stream_cast(src_hbm, row_chunk, store, row0=0, rows=None):
    rows = src_hbm.shape[0] - row0 if rows is None else rows
    width = src_hbm.shape[1]
    n = rows // row_chunk
    assert n * row_chunk == rows

    def body(stage, sem):
        def copy(k):
            slot = k % STAGE_SLOTS
            src = src_hbm.at[pl.ds(row0 + k * row_chunk, row_chunk), :]
            return pltpu.make_async_copy(src, stage.at[slot], sem.at[slot])
        for k in range(min(STAGE_SLOTS - 1, n)):
            copy(k).start()
        for k in range(n):
            if k + STAGE_SLOTS - 1 < n:
                copy(k + STAGE_SLOTS - 1).start()
            copy(k).wait()
            store(k, stage[k % STAGE_SLOTS])

    pl.run_scoped(body, pltpu.VMEM((STAGE_SLOTS, row_chunk, width), F32),
                  pltpu.SemaphoreType.DMA((STAGE_SLOTS,)))


def _cast_into(dst_ref, row_chunk):
    def store(k, chunk):
        dst_ref[k * row_chunk:(k + 1) * row_chunk, :] = chunk.astype(dst_ref.dtype)
    return store


def _ada_kernel(c_ref, w_ref, b_ref, o_ref):
    cond = _silu(c_ref[...])
    cond8 = jnp.broadcast_to(cond, (SUBLANES, cond.shape[1])).astype(BF16)
    o_ref[...] = _dot(cond8, w_ref[...].astype(BF16))[0:1] + b_ref[...]


def _ada(c, w_ada, b_ada):
    d, n = w_ada.shape
    blk = d
    return pl.pallas_call(
        _ada_kernel,
        grid=(n // blk,),
        in_specs=[pl.BlockSpec((1, d), lambda j: (0, 0)),
                  pl.BlockSpec((d, blk), lambda j: (0, j)),
                  pl.BlockSpec((1, blk), lambda j: (0, j))],
        out_specs=pl.BlockSpec((1, blk), lambda j: (0, j)),
        out_shape=jax.ShapeDtypeStruct((1, n), F32),
        compiler_params=_params(),
        name="ada",
    )(c, w_ada, b_ada.reshape(1, n))


def _modulated_rmsnorm(x, g, shift, scale):
    y = x * lax.rsqrt(jnp.mean(x * x, axis=-1, keepdims=True) + EPS) * g
    return y * (1.0 + scale) + shift


def _split3(v):
    hi = v.astype(BF16)
    r = v - hi.astype(F32)
    mid = r.astype(BF16)
    lo = (r - mid.astype(F32)).astype(BF16)
    return hi, mid, lo


def _conv_silu_block(xblk, halo_ref, xc_ref, cw_ref, cb_ref, cs):
    rows = xblk.shape[0]
    xb = jnp.concatenate([halo_ref[:, cs], xblk], axis=0)
    acc = cb_ref[:, cs] + cw_ref[SSD_CONV - 1:SSD_CONV, cs] * xblk
    for shift in range(1, SSD_CONV):
        k = SSD_CONV - 1 - shift
        acc = acc + cw_ref[k:k + 1, cs] * pltpu.roll(xb, shift, axis=0)[SUBLANES:]
    xc_ref[:, cs] = _silu(acc)
    halo_ref[:, cs] = xblk[rows - SUBLANES:]


def _ssd_chunk(xc_ref, r0, dtv, alog_ref, dsk_ref, st_ref, y_ref, fillers, *, inner, heads):
    L = SSD_CHUNK
    N = SSD_STATE
    P = SSD_HEAD_DIM
    hpg = heads // SSD_GROUPS
    rs = slice(r0, r0 + L)

    da = dtv * (-jnp.exp(alog_ref[...]))
    row = lax.broadcasted_iota(jnp.int32, (L, L), 0)
    col = lax.broadcasted_iota(jnp.int32, (L, L), 1)
    causal = row >= col
    tril = jnp.where(causal, 1.0, 0.0).astype(BF16)
    hi, mid, lo = _split3(da)
    acum = (_dot(tril, hi) + _dot(tril, mid) + _dot(tril, lo)) * LOG2E
    src_t = (acum - jnp.log2(dtv)).T
    last = acum[L - 1:L, :]
    wgt = dtv * jnp.exp2(last - acum)
    ea_last = jnp.exp2(last)

    lane = lax.broadcasted_iota(jnp.int32, (L, 2 * LANES), 1)
    lane1 = lax.broadcasted_iota(jnp.int32, (L, LANES), 1)
    lane_row = lax.broadcasted_iota(jnp.int32, (1, LANES), 1)

    for g in range(SSD_GROUPS):
        b_g = xc_ref[rs, inner + g * N: inner + (g + 1) * N]
        c_g = xc_ref[rs, inner + (SSD_GROUPS + g) * N: inner + (SSD_GROUPS + g + 1) * N]
        cb = lax.dot_general(c_g.astype(BF16), b_g.astype(BF16),
                             (((1,), (1,)), ((), ())), preferred_element_type=F32)
        scale_rows = []
        for quad in range(hpg // 4):
            h0 = g * hpg + quad * 4
            q0 = h0 * P
            lhs = []
            for h in range(h0, h0 + 4):
                colb = jnp.broadcast_to(acum[:, h:h + 1], (L, L))
                w_h = cb * jnp.exp2(jnp.where(causal, colb - src_t[h:h + 1, :], -jnp.inf))
                e_h = c_g * jnp.exp2(colb)
                lhs.append(jnp.concatenate([w_h.astype(BF16), e_h.astype(BF16)], axis=1))
                scale_rows.append(jnp.broadcast_to(ea_last[:, h:h + 1], (1, LANES)))
            lhs = jnp.concatenate(lhs, axis=0)
            xq = xc_ref[rs, q0:q0 + 4 * P]
            rhs = jnp.concatenate([xq.astype(BF16), st_ref[:, q0:q0 + 4 * P].astype(BF16)], axis=0)
            out = _dot(lhs, rhs)
            yq = jnp.where(lane < P, out[0:L],
                           jnp.where(lane < 2 * P, out[L:2 * L],
                                     jnp.where(lane < 3 * P, out[2 * L:3 * L], out[3 * L:4 * L])))
            y_ref[rs, q0:q0 + 4 * P] = yq + dsk_ref[:, q0:q0 + 4 * P] * xq
            next(fillers, lambda: None)()
        xw = []
        for pair in range(hpg // 2):
            h0 = g * hpg + pair * 2
            p0 = h0 * P
            wb = jnp.where(lane1 < P,
                           jnp.broadcast_to(wgt[:, h0:h0 + 1], (L, LANES)),
                           jnp.broadcast_to(wgt[:, h0 + 1:h0 + 2], (L, LANES)))
            xw.append((xc_ref[rs, p0:p0 + 2 * P] * wb).astype(BF16))
            srow = jnp.where(lane_row < P, scale_rows[pair * 2], scale_rows[pair * 2 + 1])
            st_ref[:, p0:p0 + 2 * P] = st_ref[:, p0:p0 + 2 * P] * srow
        xw = jnp.concatenate(xw, axis=1)
        g0 = g * hpg * P
        st_ref[:, g0:g0 + hpg * P] += _dot(b_g.T.astype(BF16), xw)


def _pool_mix(pext_ref, wpg_ref, psc_ref, tile, step_idx):
    pw = pext_ref.shape[1]
    gd = pw // len(POOL_WINDOWS)
    t_glob = step_idx * tile + lax.broadcasted_iota(jnp.int32, (tile, 1), 0) + 1
    pooled = []
    for g, w in enumerate(POOL_WINDOWS):
        gs = slice(g * gd, (g + 1) * gd)
        acc = pext_ref[:, gs]
        step = 1
        while step < w:
            acc = acc + pltpu.roll(acc, step, axis=0)
            step *= 2
        inv_cnt = 1.0 / jnp.minimum(t_glob, w).astype(F32)
        pg = acc[POOL_HALO:, :] * inv_cnt - pext_ref[POOL_HALO:POOL_HALO + tile, gs]
        pooled.append(_dot(pg.astype(BF16), wpg_ref[g]))
    pext_ref[0:POOL_HALO, :] = pext_ref[tile:tile + POOL_HALO, :]
    return (jnp.concatenate(pooled, axis=1) * psc_ref[...]).astype(BF16)


def _mixer_kernel(x_ref, mod_ref, g1_ref, w_in_t_hbm,
                  cw_ref, cb_ref, dtb_ref, alog_ref, dsk_ref, gn_ref,
                  wso_hbm, wpg_ref, psc_ref, wpo_hbm, wo_hbm,
                  h_ref, halo_ref, xc_ref, st_ref, y_ref, pext_ref, z_ref, gt_ref,
                  wzx_ref, wdt_ref, wp_ref, wg_ref, wso_ref, wpo_ref, wo_ref, *, tile, heads):
    d = x_ref.shape[1]
    inner = y_ref.shape[1]
    L = SSD_CHUNK
    i = pl.program_id(0)

    @pl.when(i == 0)
    def _():
        halo_ref[...] = jnp.zeros(halo_ref.shape, F32)
        st_ref[...] = jnp.zeros(st_ref.shape, F32)
        pext_ref[0:POOL_HALO, :] = jnp.zeros((POOL_HALO, pext_ref.shape[1]), F32)
        chunk_rows = 2 * WEIGHT_ROWS
        n_zx, pw, gw2 = wzx_ref.shape[0], wp_ref.shape[0], wg_ref.shape[0]
        _stream_cast(w_in_t_hbm, chunk_rows, _cast_into(wzx_ref, chunk_rows), 0, n_zx)

        def store_dt(k, chunk):
            lane = lax.broadcasted_iota(jnp.int32, (chunk.shape[1], LANES), 1)
            wdt_ref[...] = jnp.where(lane < heads, chunk.T, 0.0).astype(BF16)
        _stream_cast(w_in_t_hbm, LANES, store_dt, n_zx, LANES)
        _stream_cast(w_in_t_hbm, chunk_rows, _cast_into(wp_ref, chunk_rows), n_zx + heads, pw)
        _stream_cast(w_in_t_hbm, chunk_rows, _cast_into(wg_ref, chunk_rows), n_zx + heads + pw, gw2)
        for src, dst in ((wso_hbm, wso_ref), (wpo_hbm, wpo_ref), (wo_hbm, wo_ref)):
            _stream_cast(src, chunk_rows, _cast_into(dst, chunk_rows))

    x = x_ref[...]
    u = _modulated_rmsnorm(x, g1_ref[...], mod_ref[0:1, :], mod_ref[1:2, :]).astype(BF16)

    blk = MIXER_COLS

    def proj(dst_ref, r_off, w_ref, w_off, cs):
        def run():
            dst_ref[r_off:r_off + tile, cs] = _dot_nt(u, w_ref[w_off + cs.start:w_off + cs.stop, :])
        return run

    for c0 in range(0, xc_ref.shape[1], blk):
        xblk = _dot_nt(u, wzx_ref[inner + c0:inner + c0 + blk, :])
        _conv_silu_block(xblk, halo_ref, xc_ref, cw_ref, cb_ref, slice(c0, c0 + blk))
    dtv = jax.nn.softplus(_dot(u, wdt_ref[...]) + dtb_ref[...])

    fillers = iter([proj(z_ref, 0, wzx_ref, 0, slice(c0, c0 + blk)) for c0 in range(0, inner, blk)]
                   + [proj(pext_ref, POOL_HALO, wp_ref, 0, slice(c0, c0 + blk))
                      for c0 in range(0, pext_ref.shape[1], blk)]
                   + [proj(gt_ref, 0, wg_ref, 0, slice(c0, c0 + blk)) for c0 in range(0, 2 * d, blk)])
    for c in range(tile // L):
        _ssd_chunk(xc_ref, c * L, dtv[c * L:(c + 1) * L], alog_ref, dsk_ref, st_ref, y_ref, fillers,
                   inner=inner, heads=heads)
    for run in fillers:
        run()

    gw = inner // SSD_GROUPS
    yn = []
    for g in range(SSD_GROUPS):
        gs = slice(g * gw, (g + 1) * gw)
        t = y_ref[:, gs] * _silu(z_ref[:, gs])
        ms = jnp.mean(t * t, axis=-1, keepdims=True)
        yn.append((t * lax.rsqrt(ms + EPS) * gn_ref[:, gs]).astype(BF16))
    y_a = _dot(jnp.concatenate(yn, axis=1), wso_ref[...])

    pb = _pool_mix(pext_ref, wpg_ref, psc_ref, tile, i)
    y_b = _dot(pb, wpo_ref[...])

    sg = jax.nn.sigmoid(gt_ref[...])
    merged = (sg[:, :d] * y_a + sg[:, d:] * y_b).astype(BF16)
    h_ref[...] = x + mod_ref[2:3, :] * _dot(merged, wo_ref[...])


def _mixer(x, mod, g1, w_in_t, conv_w, conv_b, dt_bias, a_log, d_skip, g_norm,
           wso, wpg, psc, wpo, wo, heads, tile):
    s, d = x.shape
    inner = heads * SSD_HEAD_DIM
    width = conv_w.shape[1]
    pw = psc.shape[1]
    kern = functools.partial(_mixer_kernel, tile=tile, heads=heads)
    hbm = pl.BlockSpec(memory_space=pl.ANY)
    small = lambda a: _resident(a.shape)
    return pl.pallas_call(
        kern,
        grid=(s // tile,),
        in_specs=[_rows(tile, d), small(mod), small(g1), hbm,
                  small(conv_w), small(conv_b), small(dt_bias), small(a_log), small(d_skip), small(g_norm),
                  hbm, small(wpg), small(psc), hbm, hbm],
        out_specs=_rows(tile, d),
        out_shape=jax.ShapeDtypeStruct((s, d), F32),
        scratch_shapes=[pltpu.VMEM((SUBLANES, width), F32),
                        pltpu.VMEM((tile, width), F32),
                        pltpu.VMEM((SSD_STATE, inner), F32),
                        pltpu.VMEM((tile, inner), F32),
                        pltpu.VMEM((tile + POOL_HALO, pw), F32),
                        pltpu.VMEM((tile, inner), F32),
                        pltpu.VMEM((tile, 2 * d), F32),
                        pltpu.VMEM((inner + width, d), BF16),
                        pltpu.VMEM((d, LANES), BF16),
                        pltpu.VMEM((pw, d), BF16),
                        pltpu.VMEM((2 * d, d), BF16),
                        pltpu.VMEM(wso.shape, BF16),
                        pltpu.VMEM(wpo.shape, BF16),
                        pltpu.VMEM(wo.shape, BF16)],
        compiler_params=_params(),
        name="mixer",
    )(x, mod, g1, w_in_t, conv_w, conv_b, dt_bias, a_log, d_skip, g_norm, wso, wpg, psc, wpo, wo)


def _ffn_kernel(h_ref, mod_ref, g2_ref, win_hbm, cw_ref, cb_ref, wout_hbm, gf_ref,
                o_ref, gpad_ref, act_ref, win_ref, wout_ref, *, tile):
    f = wout_ref.shape[0]
    blk = FFN_COLS

    @pl.when(pl.program_id(0) == 0)
    def _():
        gpad_ref[0:SUBLANES, :] = jnp.zeros((SUBLANES, f), F32)
        in_rows = WEIGHT_ROWS // 2
        _stream_cast(win_hbm, in_rows, _cast_into(win_ref, in_rows))
        out_rows = f // 8
        _stream_cast(wout_hbm, out_rows, _cast_into(wout_ref, out_rows))

    h = h_ref[...]
    v = _modulated_rmsnorm(h, g2_ref[...], mod_ref[3:4, :], mod_ref[4:5, :]).astype(BF16)
    blocks = [slice(c0, min(c0 + blk, f)) for c0 in range(0, f, blk)]

    def gate_proj(cs):
        gpad_ref[SUBLANES:SUBLANES + tile, cs] = _dot(v, win_ref[:, f + cs.start:f + cs.stop])

    def conv_glu(cs):
        gb = gpad_ref[0:tile + SUBLANES, cs]
        acc = cb_ref[:, cs] + cw_ref[FFN_CONV - 1:FFN_CONV, cs] * gb[SUBLANES:]
        for shift in range(1, FFN_CONV):
            k = FFN_CONV - 1 - shift
            acc = acc + cw_ref[k:k + 1, cs] * pltpu.roll(gb, shift, axis=0)[SUBLANES:]
        gpad_ref[0:SUBLANES, cs] = gpad_ref[tile:tile + SUBLANES, cs]
        act_ref[:, cs] = (_silu(acc) * _dot(v, win_ref[:, cs])).astype(BF16)

    for j, cs in enumerate(blocks):
        gate_proj(cs)
        if j > 0:
            conv_glu(blocks[j - 1])
    conv_glu(blocks[-1])
    h2 = h + mod_ref[5:6, :] * _dot(act_ref[...], wout_ref[...])
    o_ref[...] = h2 * lax.rsqrt(jnp.mean(h2 * h2, axis=-1, keepdims=True) + EPS) * gf_ref[...]


def _ffn(h, mod, g2, win, conv_w, conv_b, wout, gf, tile):
    s, d = h.shape
    f = wout.shape[0]
    kern = functools.partial(_ffn_kernel, tile=tile)
    hbm = pl.BlockSpec(memory_space=pl.ANY)
    small = lambda a: _resident(a.shape)
    return pl.pallas_call(
        kern,
        grid=(s // tile,),
        in_specs=[_rows(tile, d), small(mod), small(g2), hbm, small(conv_w), small(conv_b), hbm, small(gf)],
        out_specs=_rows(tile, d),
        out_shape=jax.ShapeDtypeStruct((s, d), F32),
        scratch_shapes=[pltpu.VMEM((tile + SUBLANES, f), F32),
                        pltpu.VMEM((tile, f), BF16),
                        pltpu.VMEM(win.shape, BF16),
                        pltpu.VMEM(wout.shape, BF16)],
        compiler_params=_params(),
        name="ffn",
    )(h, mod, g2, win, conv_w, conv_b, wout, gf)


def _layer(h, c, w_ada, b_ada, g_norm1, w_in, ssd_conv_w, ssd_conv_b, ssd_dt_bias, ssd_a_log,
           ssd_d, g_ssd_norm, w_ssd_out, w_pool_grp, pool_scale, w_pool_out, w_out, g_norm2,
           w_ffn_in, ffn_conv_w, ffn_conv_b, w_ffn_out, g_final):
    s, d = h.shape
    heads = ssd_dt_bias.shape[0]
    inner = heads * SSD_HEAD_DIM
    conv_dim = ssd_conv_w.shape[1]
    pool_w = pool_scale.shape[0]
    f = ffn_conv_b.shape[0]
    mod = _ada(c, w_ada, b_ada).reshape(N_MOD, d)

    pad_h = lambda a: jnp.pad(a, (0, LANES - heads)).reshape(1, LANES)
    h1 = _mixer(h, mod, g_norm1.reshape(1, d), w_in.T,
                ssd_conv_w, ssd_conv_b.reshape(1, conv_dim), pad_h(ssd_dt_bias), pad_h(ssd_a_log),
                jnp.repeat(ssd_d, SSD_HEAD_DIM).reshape(1, inner), g_ssd_norm.reshape(1, inner),
                w_ssd_out, w_pool_grp.astype(BF16), pool_scale.reshape(1, pool_w), w_pool_out, w_out,
                heads, tile=MIXER_TILE)
    return _ffn(h1, mod, g_norm2.reshape(1, d), w_ffn_in, ffn_conv_w, ffn_conv_b.reshape(1, f),
                w_ffn_out, g_final.reshape(1, d), tile=FFN_TILE)


def kernel(x, c, w_ada, b_ada, g_norm1, w_in, ssd_conv_w, ssd_conv_b, ssd_dt_bias, ssd_a_log, ssd_d,
           g_ssd_norm, w_ssd_out, w_pool_grp, pool_scale, w_pool_out, w_out, g_norm2, w_ffn_in,
           ffn_conv_w, ffn_conv_b, w_ffn_out, g_final):
    assert x.shape[0] == 1 and w_ada.shape[0] == 1, "one sequence, one layer"
    out = _layer(x[0], c, w_ada[0], b_ada[0], g_norm1[0], w_in[0], ssd_conv_w[0], ssd_conv_b[0],
                 ssd_dt_bias[0], ssd_a_log[0], ssd_d[0], g_ssd_norm[0], w_ssd_out[0], w_pool_grp[0],
                 pool_scale[0], w_pool_out[0], w_out[0], g_norm2[0], w_ffn_in[0], ffn_conv_w[0],
                 ffn_conv_b[0], w_ffn_out[0], g_final)
    return out[None]
```

```python
import functools

import jax
import jax.numpy as jnp
from jax import lax
from jax.experimental import pallas as pl
from jax.experimental.pallas import tpu as pltpu

F32 = jnp.float32
BF16 = jnp.bfloat16

EPS = 1e-6
LOG2E = 1.4426950408889634
SSD_HEAD_DIM = 64
SSD_GROUPS = 4
SSD_STATE = 128
SSD_CONV = 4
SSD_CHUNK = 128
POOL_WINDOWS = (2, 4, 8, 16)
POOL_HALO = 16
FFN_CONV = 3
LANES = 128
SUBLANES = 8
VMEM_LIMIT = 56 * 1024 * 1024
MIXER_TILE = 256
FFN_TILE = 512
MIXER_COLS = 4 * LANES
FFN_COLS = 2 * LANES
WEIGHT_ROWS = 128
STAGE_SLOTS = 4


def _dot(a, b):
    return jnp.dot(a, b, preferred_element_type=F32)


def _dot_nt(a, b):
    return lax.dot_general(a, b, (((1,), (1,)), ((), ())), preferred_element_type=F32)


def _silu(v):
    return v * jax.nn.sigmoid(v)


def _resident(shape):
    return pl.BlockSpec(shape, lambda i: (0,) * len(shape), pipeline_mode=pl.Buffered(1))


def _rows(tile, width):
    return pl.BlockSpec((tile, width), lambda i: (i, 0))


def _params(vmem=VMEM_LIMIT):
    return pltpu.CompilerParams(dimension_semantics=("arbitrary",), vmem_limit_bytes=vmem)


def _stream_cast(src_hbm, row_chunk, store, row0=0, rows=None):
    rows = src_hbm.shape[0] - row0 if rows is None else rows
    width = src_hbm.shape[1]
    n = rows // row_chunk
    assert n * row_chunk == rows

    def body(stage, sem):
        def copy(k):
            slot = k % STAGE_SLOTS
            src = src_hbm.at[pl.ds(row0 + k * row_chunk, row_chunk), :]
            return pltpu.make_async_copy(src, stage.at[slot], sem.at[slot])
        for k in range(min(STAGE_SLOTS - 1, n)):
            copy(k).start()
        for k in range(n):
            if k + STAGE_SLOTS - 1 < n:
                copy(k + STAGE_SLOTS - 1).start()
            copy(k).wait()
            store(k, stage[k % STAGE_SLOTS])

    pl.run_scoped(body, pltpu.VMEM((STAGE_SLOTS, row_chunk, width), F32),
                  pltpu.SemaphoreType.DMA((STAGE_SLOTS,)))


def _cast_into(dst_ref, row_chunk):
    def store(k, chunk):
        dst_ref[k * row_chunk:(k + 1) * row_chunk, :] = chunk.astype(dst_ref.dtype)
    return store


def _ada_kernel(c_ref, w_ref, b_ref, o_ref):
    cond = _silu(c_ref[...])
    cond8 = jnp.broadcast_to(cond, (SUBLANES, cond.shape[1])).astype(BF16)
    o_ref[0] = _dot(cond8, w_ref[...].astype(BF16))[0:1] + b_ref[...]


def _ada(c, w_ada, b_ada):
    d, n = w_ada.shape
    return pl.pallas_call(
        _ada_kernel,
        grid=(n // d,),
        in_specs=[pl.BlockSpec((1, d), lambda j: (0, 0)),
                  pl.BlockSpec((d, d), lambda j: (0, j)),
                  pl.BlockSpec((1, d), lambda j: (0, j))],
        out_specs=pl.BlockSpec((1, 1, d), lambda j: (j, 0, 0)),
        out_shape=jax.ShapeDtypeStruct((n // d, 1, d), F32),
        compiler_params=_params(),
        name="ada",
    )(c, w_ada, b_ada.reshape(1, n))


def _modulated_rmsnorm(x, g, shift, scale):
    y = x * lax.rsqrt(jnp.mean(x * x, axis=-1, keepdims=True) + EPS) * g
    return y * (1.0 + scale) + shift


def _split3(v):
    hi = v.astype(BF16)
    r = v - hi.astype(F32)
    mid = r.astype(BF16)
    lo = (r - mid.astype(F32)).astype(BF16)
    return hi, mid, lo


def _conv_silu_block(xblk, halo_ref, xc_ref, cw_ref, cb_ref, cs):
    rows = xblk.shape[0]
    xb = jnp.concatenate([halo_ref[:, cs], xblk], axis=0)
    acc = cb_ref[:, cs] + cw_ref[SSD_CONV - 1:SSD_CONV, cs] * xblk
    for shift in range(1, SSD_CONV):
        k = SSD_CONV - 1 - shift
        acc = acc + cw_ref[k:k + 1, cs] * pltpu.roll(xb, shift, axis=0)[SUBLANES:]
    xc_ref[:, cs] = _silu(acc)
    halo_ref[:, cs] = xblk[rows - SUBLANES:]


def _ssd_chunk(xc_ref, r0, dtv, alog, dsk_ref, st_ref, y_ref, fillers, *, inner, heads):
    L = SSD_CHUNK
    N = SSD_STATE
    P = SSD_HEAD_DIM
    hpg = heads // SSD_GROUPS
    rs = slice(r0, r0 + L)

    da = dtv * (-jnp.exp(alog))
    row = lax.broadcasted_iota(jnp.int32, (L, L), 0)
    col = lax.broadcasted_iota(jnp.int32, (L, L), 1)
    causal = row >= col
    tril = jnp.where(causal, 1.0, 0.0).astype(BF16)
    hi, mid, lo = _split3(da)
    acum = (_dot(tril, hi) + _dot(tril, mid) + _dot(tril, lo)) * LOG2E
    src_t = (acum - jnp.log2(dtv)).T
    last = acum[L - 1:L, :]
    wgt = dtv * jnp.exp2(last - acum)
    ea_last = jnp.exp2(last)

    lane = lax.broadcasted_iota(jnp.int32, (L, 2 * LANES), 1)
    lane1 = lax.broadcasted_iota(jnp.int32, (L, LANES), 1)
    lane_row = lax.broadcasted_iota(jnp.int32, (1, LANES), 1)

    for g in range(SSD_GROUPS):
        b_g = xc_ref[rs, inner + g * N: inner + (g + 1) * N]
        c_g = xc_ref[rs, inner + (SSD_GROUPS + g) * N: inner + (SSD_GROUPS + g + 1) * N]
        cb = lax.dot_general(c_g.astype(BF16), b_g.astype(BF16),
                             (((1,), (1,)), ((), ())), preferred_element_type=F32)
        scale_rows = []
        for quad in range(hpg // 4):
            h0 = g * hpg + quad * 4
            q0 = h0 * P
            lhs = []
            for h in range(h0, h0 + 4):
                colb = jnp.broadcast_to(acum[:, h:h + 1], (L, L))
                w_h = cb * jnp.exp2(jnp.where(causal, colb - src_t[h:h + 1, :], -jnp.inf))
                e_h = c_g * jnp.exp2(colb)
                lhs.append(jnp.concatenate([w_h.astype(BF16), e_h.astype(BF16)], axis=1))
                scale_rows.append(jnp.broadcast_to(ea_last[:, h:h + 1], (1, LANES)))
            lhs = jnp.concatenate(lhs, axis=0)
            xq = xc_ref[rs, q0:q0 + 4 * P]
            rhs = jnp.concatenate([xq.astype(BF16), st_ref[:, q0:q0 + 4 * P].astype(BF16)], axis=0)
            out = _dot(lhs, rhs)
            yq = jnp.where(lane < P, out[0:L],
                           jnp.where(lane < 2 * P, out[L:2 * L],
                                     jnp.where(lane < 3 * P, out[2 * L:3 * L], out[3 * L:4 * L])))
            y_ref[rs, q0:q0 + 4 * P] = yq + dsk_ref[:, q0:q0 + 4 * P] * xq
            next(fillers, lambda: None)()
        xw = []
        for pair in range(hpg // 2):
            h0 = g * hpg + pair * 2
            p0 = h0 * P
            wb = jnp.where(lane1 < P,
                           jnp.broadcast_to(wgt[:, h0:h0 + 1], (L, LANES)),
                           jnp.broadcast_to(wgt[:, h0 + 1:h0 + 2], (L, LANES)))
            xw.append((xc_ref[rs, p0:p0 + 2 * P] * wb).astype(BF16))
            srow = jnp.where(lane_row < P, scale_rows[pair * 2], scale_rows[pair * 2 + 1])
            st_ref[:, p0:p0 + 2 * P] = st_ref[:, p0:p0 + 2 * P] * srow
        xw = jnp.concatenate(xw, axis=1)
        g0 = g * hpg * P
        st_ref[:, g0:g0 + hpg * P] += _dot(b_g.T.astype(BF16), xw)


def _pool_mix(pext_ref, wpg_ref, psc_ref, tile, step_idx):
    pw = pext_ref.shape[1]
    gd = pw // len(POOL_WINDOWS)
    t_glob = step_idx * tile + lax.broadcasted_iota(jnp.int32, (tile, 1), 0) + 1
    pooled = []
    for g, w in enumerate(POOL_WINDOWS):
        gs = slice(g * gd, (g + 1) * gd)
        acc = pext_ref[:, gs]
        step = 1
        while step < w:
            acc = acc + pltpu.roll(acc, step, axis=0)
            step *= 2
        inv_cnt = 1.0 / jnp.minimum(t_glob, w).astype(F32)
        pg = acc[POOL_HALO:, :] * inv_cnt - pext_ref[POOL_HALO:POOL_HALO + tile, gs]
        pooled.append(_dot(pg.astype(BF16), wpg_ref[g]))
    pext_ref[0:POOL_HALO, :] = pext_ref[tile:tile + POOL_HALO, :]
    return (jnp.concatenate(pooled, axis=1) * psc_ref[...]).astype(BF16)


def _mixer_kernel(x_ref, mod_ref, g1_ref, w_in_t_hbm,
                  cw_ref, cb_ref, dtb_ref, alog_ref, dsk_ref, gn_ref,
                  wso_hbm, wpg_ref, psc_ref, wpo_hbm, wo_hbm,
                  h_ref, halo_ref, xc_ref, st_ref, y_ref, pext_ref, z_ref, gt_ref,
                  wzx_ref, wdt_ref, wp_ref, wg_ref, wso_ref, wpo_ref, wo_ref, *, tile, heads):
    d = x_ref.shape[1]
    inner = y_ref.shape[1]
    L = SSD_CHUNK
    i = pl.program_id(0)

    @pl.when(i == 0)
    def _():
        halo_ref[...] = jnp.zeros(halo_ref.shape, F32)
        st_ref[...] = jnp.zeros(st_ref.shape, F32)
        pext_ref[0:POOL_HALO, :] = jnp.zeros((POOL_HALO, pext_ref.shape[1]), F32)
        chunk_rows = 2 * WEIGHT_ROWS
        n_zx, pw, gw2 = wzx_ref.shape[0], wp_ref.shape[0], wg_ref.shape[0]
        _stream_cast(w_in_t_hbm, chunk_rows, _cast_into(wzx_ref, chunk_rows), 0, n_zx)

        def store_dt(k, chunk):
            lane = lax.broadcasted_iota(jnp.int32, (chunk.shape[1], LANES), 1)
            wdt_ref[...] = jnp.where(lane < heads, chunk.T, 0.0).astype(BF16)
        _stream_cast(w_in_t_hbm, LANES, store_dt, n_zx, LANES)
        _stream_cast(w_in_t_hbm, chunk_rows, _cast_into(wp_ref, chunk_rows), n_zx + heads, pw)
        _stream_cast(w_in_t_hbm, chunk_rows, _cast_into(wg_ref, chunk_rows), n_zx + heads + pw, gw2)
        for src, dst in ((wso_hbm, wso_ref), (wpo_hbm, wpo_ref), (wo_hbm, wo_ref)):
            _stream_cast(src, chunk_rows, _cast_into(dst, chunk_rows))

    x = x_ref[...]
    u = _modulated_rmsnorm(x, g1_ref[...], mod_ref[0], mod_ref[1]).astype(BF16)

    blk = MIXER_COLS

    def proj(dst_ref, r_off, w_ref, w_off, cs):
        def run():
            dst_ref[r_off:r_off + tile, cs] = _dot_nt(u, w_ref[w_off + cs.start:w_off + cs.stop, :])
        return run

    for c0 in range(0, xc_ref.shape[1], blk):
        xblk = _dot_nt(u, wzx_ref[inner + c0:inner + c0 + blk, :])
        _conv_silu_block(xblk, halo_ref, xc_ref, cw_ref, cb_ref, slice(c0, c0 + blk))
    pad = jnp.zeros((1, LANES - heads), F32)
    dtb = jnp.concatenate([dtb_ref[...], pad], axis=1)
    alog = jnp.concatenate([alog_ref[...], pad], axis=1)
    dtv = jax.nn.softplus(_dot(u, wdt_ref[...]) + dtb)

    fillers = iter([proj(z_ref, 0, wzx_ref, 0, slice(c0, c0 + blk)) for c0 in range(0, inner, blk)]
                   + [proj(pext_ref, POOL_HALO, wp_ref, 0, slice(c0, c0 + blk))
                      for c0 in range(0, pext_ref.shape[1], blk)]
                   + [proj(gt_ref, 0, wg_ref, 0, slice(c0, c0 + blk)) for c0 in range(0, 2 * d, blk)])
    for c in range(tile // L):
        _ssd_chunk(xc_ref, c * L, dtv[c * L:(c + 1) * L], alog, dsk_ref, st_ref, y_ref, fillers,
                   inner=inner, heads=heads)
    for run in fillers:
        run()

    gw = inner // SSD_GROUPS
    yn = []
    for g in range(SSD_GROUPS):
        gs = slice(g * gw, (g + 1) * gw)
        t = y_ref[:, gs] * _silu(z_ref[:, gs])
        ms = jnp.mean(t * t, axis=-1, keepdims=True)
        yn.append((t * lax.rsqrt(ms + EPS) * gn_ref[:, gs]).astype(BF16))
    y_a = _dot(jnp.concatenate(yn, axis=1), wso_ref[...])

    pb = _pool_mix(pext_ref, wpg_ref, psc_ref, tile, i)
    y_b = _dot(pb, wpo_ref[...])

    sg = jax.nn.sigmoid(gt_ref[...])
    merged = (sg[:, :d] * y_a + sg[:, d:] * y_b).astype(BF16)
    h_ref[...] = x + mod_ref[2] * _dot(merged, wo_ref[...])


def _mixer(x, mod, g1, w_in_t, conv_w, conv_b, dt_bias, a_log, d_skip, g_norm,
           wso, wpg, psc, wpo, wo, heads, tile):
    s, d = x.shape
    inner = heads * SSD_HEAD_DIM
    width = conv_w.shape[1]
    pw = psc.shape[1]
    kern = functools.partial(_mixer_kernel, tile=tile, heads=heads)
    hbm = pl.BlockSpec(memory_space=pl.ANY)
    small = lambda a: _resident(a.shape)
    return pl.pallas_call(
        kern,
        grid=(s // tile,),
        in_specs=[_rows(tile, d), small(mod), small(g1), hbm,
                  small(conv_w), small(conv_b), small(dt_bias), small(a_log), small(d_skip), small(g_norm),
                  hbm, small(wpg), small(psc), hbm, hbm],
        out_specs=_rows(tile, d),
        out_shape=jax.ShapeDtypeStruct((s, d), F32),
        scratch_shapes=[pltpu.VMEM((SUBLANES, width), F32),
                        pltpu.VMEM((tile, width), F32),
                        pltpu.VMEM((SSD_STATE, inner), F32),
                        pltpu.VMEM((tile, inner), F32),
                        pltpu.VMEM((tile + POOL_HALO, pw), F32),
                        pltpu.VMEM((tile, inner), F32),
                        pltpu.VMEM((tile, 2 * d), F32),
                        pltpu.VMEM((inner + width, d), BF16),
                        pltpu.VMEM((d, LANES), BF16),
                        pltpu.VMEM((pw, d), BF16),
                        pltpu.VMEM((2 * d, d), BF16),
                        pltpu.VMEM(wso.shape, BF16),
                        pltpu.VMEM(wpo.shape, BF16),
                        pltpu.VMEM(wo.shape, BF16)],
        compiler_params=_params(),
        name="mixer",
    )(x, mod, g1, w_in_t, conv_w, conv_b, dt_bias, a_log, d_skip, g_norm, wso, wpg, psc, wpo, wo)


def _ffn_kernel(h_ref, mod_ref, g2_ref, win_hbm, cw_ref, cb_ref, wout_hbm, gf_ref,
                o_ref, gpad_ref, act_ref, win_ref, wout_ref, *, tile):
    f = wout_ref.shape[0]
    blk = FFN_COLS

    @pl.when(pl.program_id(0) == 0)
    def _():
        gpad_ref[0:SUBLANES, :] = jnp.zeros((SUBLANES, f), F32)
        in_rows = WEIGHT_ROWS // 2
        _stream_cast(win_hbm, in_rows, _cast_into(win_ref, in_rows))
        out_rows = f // 8
        _stream_cast(wout_hbm, out_rows, _cast_into(wout_ref, out_rows))

    h = h_ref[...]
    v = _modulated_rmsnorm(h, g2_ref[...], mod_ref[3], mod_ref[4]).astype(BF16)
    blocks = [slice(c0, min(c0 + blk, f)) for c0 in range(0, f, blk)]

    def gate_proj(cs):
        gpad_ref[SUBLANES:SUBLANES + tile, cs] = _dot(v, win_ref[:, f + cs.start:f + cs.stop])

    def conv_glu(cs):
        gb = gpad_ref[0:tile + SUBLANES, cs]
        acc = cb_ref[:, cs] + cw_ref[FFN_CONV - 1:FFN_CONV, cs] * gb[SUBLANES:]
        for shift in range(1, FFN_CONV):
            k = FFN_CONV - 1 - shift
            acc = acc + cw_ref[k:k + 1, cs] * pltpu.roll(gb, shift, axis=0)[SUBLANES:]
        gpad_ref[0:SUBLANES, cs] = gpad_ref[tile:tile + SUBLANES, cs]
        act_ref[:, cs] = (_silu(acc) * _dot(v, win_ref[:, cs])).astype(BF16)

    for j, cs in enumerate(blocks):
        gate_proj(cs)
        if j > 0:
            conv_glu(blocks[j - 1])
    conv_glu(blocks[-1])
    h2 = h + mod_ref[5] * _dot(act_ref[...], wout_ref[...])
    o_ref[...] = h2 * lax.rsqrt(jnp.mean(h2 * h2, axis=-1, keepdims=True) + EPS) * gf_ref[...]


def _ffn(h, mod, g2, win, conv_w, conv_b, wout, gf, tile):
    s, d = h.shape
    f = wout.shape[0]
    kern = functools.partial(_ffn_kernel, tile=tile)
    hbm = pl.BlockSpec(memory_space=pl.ANY)
    small = lambda a: _resident(a.shape)
    return pl.pallas_call(
        kern,
        grid=(s // tile,),
        in_specs=[_rows(tile, d), small(mod), small(g2), hbm, small(conv_w), small(conv_b), hbm, small(gf)],
        out_specs=_rows(tile, d),
        out_shape=jax.ShapeDtypeStruct((s, d), F32),
        scratch_shapes=[pltpu.VMEM((tile + SUBLANES, f), F32),
                        pltpu.VMEM((tile, f), BF16),
                        pltpu.VMEM(win.shape, BF16),
                        pltpu.VMEM(wout.shape, BF16)],
        compiler_params=_params(),
        name="ffn",
    )(h, mod, g2, win, conv_w, conv_b, wout, gf)


def _layer(h, c, w_ada, b_ada, g_norm1, w_in, ssd_conv_w, ssd_conv_b, ssd_dt_bias, ssd_a_log,
           ssd_d, g_ssd_norm, w_ssd_out, w_pool_grp, pool_scale, w_pool_out, w_out, g_norm2,
           w_ffn_in, ffn_conv_w, ffn_conv_b, w_ffn_out, g_final):
    s, d = h.shape
    heads = ssd_dt_bias.shape[0]
    inner = heads * SSD_HEAD_DIM
    conv_dim = ssd_conv_w.shape[1]
    pool_w = pool_scale.shape[0]
    f = ffn_conv_b.shape[0]
    mod = _ada(c, w_ada, b_ada)

    h1 = _mixer(h, mod, g_norm1.reshape(1, d), w_in.T,
                ssd_conv_w, ssd_conv_b.reshape(1, conv_dim),
                ssd_dt_bias.reshape(1, heads), ssd_a_log.reshape(1, heads),
                jnp.repeat(ssd_d, SSD_HEAD_DIM).reshape(1, inner), g_ssd_norm.reshape(1, inner),
                w_ssd_out, w_pool_grp.astype(BF16), pool_scale.reshape(1, pool_w), w_pool_out, w_out,
                heads, tile=MIXER_TILE)
    return _ffn(h1, mod, g_norm2.reshape(1, d), w_ffn_in, ffn_conv_w, ffn_conv_b.reshape(1, f),
                w_ffn_out, g_final.reshape(1, d), tile=FFN_TILE)


def kernel(x, c, w_ada, b_ada, g_norm1, w_in, ssd_conv_w, ssd_conv_b, ssd_dt_bias, ssd_a_log, ssd_d,
           g_ssd_norm, w_ssd_out, w_pool_grp, pool_scale, w_pool_out, w_out, g_norm2, w_ffn_in,
           ffn_conv_w, ffn_conv_b, w_ffn_out, g_final):
    assert x.shape[0] == 1 and w_ada.shape[0] == 1, "one sequence, one layer"
    out = _layer(x[0], c, w_ada[0], b_ada[0], g_norm1[0], w_in[0], ssd_conv_w[0], ssd_conv_b[0],
                 ssd_dt_bias[0], ssd_a_log[0], ssd_d[0], g_ssd_norm[0], w_ssd_out[0], w_pool_grp[0],
                 pool_scale[0], w_pool_out[0], w_out[0], g_norm2[0], w_ffn_in[0], ffn_conv_w[0],
                 ffn_conv_b[0], w_ffn_out[0], g_final)
    return out[None]
```

```python
import functools

import jax
import jax.numpy as jnp
from jax import lax
from jax.experimental import pallas as pl
from jax.experimental.pallas import tpu as pltpu

F32 = jnp.float32
BF16 = jnp.bfloat16

EPS = 1e-6
LOG2E = 1.4426950408889634
SSD_HEAD_DIM = 64
SSD_GROUPS = 4
SSD_STATE = 128
SSD_CONV = 4
SSD_CHUNK = 128
POOL_WINDOWS = (2, 4, 8, 16)
POOL_HALO = 16
FFN_CONV = 3
LANES = 128
SUBLANES = 8
VMEM_LIMIT = 56 * 1024 * 1024
MIXER_TILE = 256
FFN_TILE = 512
MIXER_COLS = 4 * LANES
FFN_COLS = 2 * LANES
WEIGHT_ROWS = 128
STAGE_SLOTS = 8


def _dot(a, b):
    return jnp.dot(a, b, preferred_element_type=F32)


def _dot_nt(a, b):
    return lax.dot_general(a, b, (((1,), (1,)), ((), ())), preferred_element_type=F32)


def _silu(v):
    return v * jax.nn.sigmoid(v)


def _resident(shape):
    return pl.BlockSpec(shape, lambda i: (0,) * len(shape), pipeline_mode=pl.Buffered(1))


def _rows(tile, width):
    return pl.BlockSpec((tile, width), lambda i: (i, 0))


def _params(vmem=VMEM_LIMIT):
    return pltpu.CompilerParams(dimension_semantics=("arbitrary",), vmem_limit_bytes=vmem)


def _stream_cast(src_hbm, row_chunk, store, row0=0, rows=None):
    rows = src_hbm.shape[0] - row0 if rows is None else rows
    width = src_hbm.shape[1]
    n = rows // row_chunk
    assert n * row_chunk == rows

    def body(stage, sem):
        def copy(k):
            slot = k % STAGE_SLOTS
            src = src_hbm.at[pl.ds(row0 + k * row_chunk, row_chunk), :]
            return pltpu.make_async_copy(src, stage.at[slot], sem.at[slot])
        for k in range(min(STAGE_SLOTS - 1, n)):
            copy(k).start()
        for k in range(n):
            if k + STAGE_SLOTS - 1 < n:
                copy(k + STAGE_SLOTS - 1).start()
            copy(k).wait()
            store(k, stage[k % STAGE_SLOTS])

    pl.run_scoped(body, pltpu.VMEM((STAGE_SLOTS, row_chunk, width), F32),
                  pltpu.SemaphoreType.DMA((STAGE_SLOTS,)))


def _cast_into(dst_ref, row_chunk):
    def store(k, chunk):
        dst_ref[k * row_chunk:(k + 1) * row_chunk, :] = chunk.astype(dst_ref.dtype)
    return store


def _ada_kernel(c_ref, w_ref, b_ref, o_ref):
    cond = _silu(c_ref[...])
    cond8 = jnp.broadcast_to(cond, (SUBLANES, cond.shape[1])).astype(BF16)
    o_ref[0] = _dot(cond8, w_ref[...].astype(BF16))[0:1] + b_ref[...]


def _ada(c, w_ada, b_ada):
    d, n = w_ada.shape
    return pl.pallas_call(
        _ada_kernel,
        grid=(n // d,),
        in_specs=[pl.BlockSpec((1, d), lambda j: (0, 0)),
                  pl.BlockSpec((d, d), lambda j: (0, j)),
                  pl.BlockSpec((1, d), lambda j: (0, j))],
        out_specs=pl.BlockSpec((1, 1, d), lambda j: (j, 0, 0)),
        out_shape=jax.ShapeDtypeStruct((n // d, 1, d), F32),
        compiler_params=_params(),
        name="ada",
    )(c, w_ada, b_ada.reshape(1, n))


def _modulated_rmsnorm(x, g, shift, scale):
    y = x * lax.rsqrt(jnp.mean(x * x, axis=-1, keepdims=True) + EPS) * g
    return y * (1.0 + scale) + shift


def _split3(v):
    hi = v.astype(BF16)
    r = v - hi.astype(F32)
    mid = r.astype(BF16)
    lo = (r - mid.astype(F32)).astype(BF16)
    return hi, mid, lo


def _conv_silu_block(xblk, halo_ref, xc_ref, cw_ref, cb_ref, cs):
    rows = xblk.shape[0]
    xb = jnp.concatenate([halo_ref[:, cs], xblk], axis=0)
    acc = cb_ref[:, cs] + cw_ref[SSD_CONV - 1:SSD_CONV, cs] * xblk
    for shift in range(1, SSD_CONV):
        k = SSD_CONV - 1 - shift
        acc = acc + cw_ref[k:k + 1, cs] * pltpu.roll(xb, shift, axis=0)[SUBLANES:]
    xc_ref[:, cs] = _silu(acc)
    halo_ref[:, cs] = xblk[rows - SUBLANES:]


def _ssd_chunk(xc_ref, r0, dtv, alog, dsk_ref, st_ref, y_ref, fillers, *, inner, heads):
    L = SSD_CHUNK
    N = SSD_STATE
    P = SSD_HEAD_DIM
    hpg = heads // SSD_GROUPS
    rs = slice(r0, r0 + L)

    da = dtv * (-jnp.exp(alog))
    row = lax.broadcasted_iota(jnp.int32, (L, L), 0)
    col = lax.broadcasted_iota(jnp.int32, (L, L), 1)
    causal = row >= col
    tril = jnp.where(causal, 1.0, 0.0).astype(BF16)
    hi, mid, lo = _split3(da)
    acum = (_dot(tril, hi) + _dot(tril, mid) + _dot(tril, lo)) * LOG2E
    src_t = (acum - jnp.log2(dtv)).T
    last = acum[L - 1:L, :]
    wgt = dtv * jnp.exp2(last - acum)
    ea_last = jnp.exp2(last)

    lane = lax.broadcasted_iota(jnp.int32, (L, 2 * LANES), 1)
    lane1 = lax.broadcasted_iota(jnp.int32, (L, LANES), 1)
    lane_row = lax.broadcasted_iota(jnp.int32, (1, LANES), 1)

    for g in range(SSD_GROUPS):
        b_g = xc_ref[rs, inner + g * N: inner + (g + 1) * N]
        c_g = xc_ref[rs, inner + (SSD_GROUPS + g) * N: inner + (SSD_GROUPS + g + 1) * N]
        cb = lax.dot_general(c_g.astype(BF16), b_g.astype(BF16),
                             (((1,), (1,)), ((), ())), preferred_element_type=F32)
        scale_rows = []
        for quad in range(hpg // 4):
            h0 = g * hpg + quad * 4
            q0 = h0 * P
            lhs = []
            for h in range(h0, h0 + 4):
                colb = jnp.broadcast_to(acum[:, h:h + 1], (L, L))
                w_h = cb * jnp.exp2(jnp.where(causal, colb - src_t[h:h + 1, :], -jnp.inf))
                e_h = c_g * jnp.exp2(colb)
                lhs.append(jnp.concatenate([w_h.astype(BF16), e_h.astype(BF16)], axis=1))
                scale_rows.append(jnp.broadcast_to(ea_last[:, h:h + 1], (1, LANES)))
            lhs = jnp.concatenate(lhs, axis=0)
            xq = xc_ref[rs, q0:q0 + 4 * P]
            rhs = jnp.concatenate([xq.astype(BF16), st_ref[:, q0:q0 + 4 * P].astype(BF16)], axis=0)
            out = _dot(lhs, rhs)
            yq = jnp.where(lane < P, out[0:L],
                           jnp.where(lane < 2 * P, out[L:2 * L],
                                     jnp.where(lane < 3 * P, out[2 * L:3 * L], out[3 * L:4 * L])))
            y_ref[rs, q0:q0 + 4 * P] = yq + dsk_ref[:, q0:q0 + 4 * P] * xq
            next(fillers, lambda: None)()
        xw = []
        for pair in range(hpg // 2):
            h0 = g * hpg + pair * 2
            p0 = h0 * P
            wb = jnp.where(lane1 < P,
                           jnp.broadcast_to(wgt[:, h0:h0 + 1], (L, LANES)),
                           jnp.broadcast_to(wgt[:, h0 + 1:h0 + 2], (L, LANES)))
            xw.append((xc_ref[rs, p0:p0 + 2 * P] * wb).astype(BF16))
            srow = jnp.where(lane_row < P, scale_rows[pair * 2], scale_rows[pair * 2 + 1])
            st_ref[:, p0:p0 + 2 * P] = st_ref[:, p0:p0 + 2 * P] * srow
        xw = jnp.concatenate(xw, axis=1)
        g0 = g * hpg * P
        st_ref[:, g0:g0 + hpg * P] += _dot(b_g.T.astype(BF16), xw)


def _pool_mix(pext_ref, wpg_ref, psc_ref, tile, step_idx):
    pw = pext_ref.shape[1]
    gd = pw // len(POOL_WINDOWS)
    t_glob = step_idx * tile + lax.broadcasted_iota(jnp.int32, (tile, 1), 0) + 1
    pooled = []
    for g, w in enumerate(POOL_WINDOWS):
        gs = slice(g * gd, (g + 1) * gd)
        acc = pext_ref[:, gs]
        step = 1
        while step < w:
            acc = acc + pltpu.roll(acc, step, axis=0)
            step *= 2
        inv_cnt = 1.0 / jnp.minimum(t_glob, w).astype(F32)
        pg = acc[POOL_HALO:, :] * inv_cnt - pext_ref[POOL_HALO:POOL_HALO + tile, gs]
        pooled.append(_dot(pg.astype(BF16), wpg_ref[g]))
    pext_ref[0:POOL_HALO, :] = pext_ref[tile:tile + POOL_HALO, :]
    return (jnp.concatenate(pooled, axis=1) * psc_ref[...]).astype(BF16)


def _mixer_kernel(x_ref, mod_ref, g1_ref, w_in_t_hbm,
                  cw_ref, cb_ref, dtb_ref, alog_ref, dsk_ref, gn_ref,
                  wso_hbm, wpg_ref, psc_ref, wpo_hbm, wo_hbm,
                  h_ref, halo_ref, xc_ref, st_ref, y_ref, pext_ref, z_ref, gt_ref,
                  wzx_ref, wdt_ref, wp_ref, wg_ref, wso_ref, wpo_ref, wo_ref, *, tile, heads):
    d = x_ref.shape[1]
    inner = y_ref.shape[1]
    L = SSD_CHUNK
    i = pl.program_id(0)

    @pl.when(i == 0)
    def _():
        halo_ref[...] = jnp.zeros(halo_ref.shape, F32)
        st_ref[...] = jnp.zeros(st_ref.shape, F32)
        pext_ref[0:POOL_HALO, :] = jnp.zeros((POOL_HALO, pext_ref.shape[1]), F32)
        chunk_rows = 2 * WEIGHT_ROWS
        n_zx, pw, gw2 = wzx_ref.shape[0], wp_ref.shape[0], wg_ref.shape[0]
        _stream_cast(w_in_t_hbm, chunk_rows, _cast_into(wzx_ref, chunk_rows), 0, n_zx)

        def store_dt(k, chunk):
            lane = lax.broadcasted_iota(jnp.int32, (chunk.shape[1], LANES), 1)
            wdt_ref[...] = jnp.where(lane < heads, chunk.T, 0.0).astype(BF16)
        _stream_cast(w_in_t_hbm, LANES, store_dt, n_zx, LANES)
        _stream_cast(w_in_t_hbm, chunk_rows, _cast_into(wp_ref, chunk_rows), n_zx + heads, pw)
        _stream_cast(w_in_t_hbm, chunk_rows, _cast_into(wg_ref, chunk_rows), n_zx + heads + pw, gw2)
        for src, dst in ((wso_hbm, wso_ref), (wpo_hbm, wpo_ref), (wo_hbm, wo_ref)):
            _stream_cast(src, chunk_rows, _cast_into(dst, chunk_rows))

    x = x_ref[...]
    u = _modulated_rmsnorm(x, g1_ref[...], mod_ref[0], mod_ref[1]).astype(BF16)

    blk = MIXER_COLS

    def proj(dst_ref, r_off, w_ref, w_off, cs):
        def run():
            dst_ref[r_off:r_off + tile, cs] = _dot_nt(u, w_ref[w_off + cs.start:w_off + cs.stop, :])
        return run

    for c0 in range(0, xc_ref.shape[1], blk):
        xblk = _dot_nt(u, wzx_ref[inner + c0:inner + c0 + blk, :])
        _conv_silu_block(xblk, halo_ref, xc_ref, cw_ref, cb_ref, slice(c0, c0 + blk))
    pad = jnp.zeros((1, LANES - heads), F32)
    dtb = jnp.concatenate([dtb_ref[...], pad], axis=1)
    alog = jnp.concatenate([alog_ref[...], pad], axis=1)
    dtv = jax.nn.softplus(_dot(u, wdt_ref[...]) + dtb)

    fillers = iter([proj(z_ref, 0, wzx_ref, 0, slice(c0, c0 + blk)) for c0 in range(0, inner, blk)]
                   + [proj(pext_ref, POOL_HALO, wp_ref, 0, slice(c0, c0 + blk))
                      for c0 in range(0, pext_ref.shape[1], blk)]
                   + [proj(gt_ref, 0, wg_ref, 0, slice(c0, c0 + blk)) for c0 in range(0, 2 * d, blk)])
    for c in range(tile // L):
        _ssd_chunk(xc_ref, c * L, dtv[c * L:(c + 1) * L], alog, dsk_ref, st_ref, y_ref, fillers,
                   inner=inner, heads=heads)
    for run in fillers:
        run()

    gw = inner // SSD_GROUPS
    yn = []
    for g in range(SSD_GROUPS):
        gs = slice(g * gw, (g + 1) * gw)
        t = y_ref[:, gs] * _silu(z_ref[:, gs])
        ms = jnp.mean(t * t, axis=-1, keepdims=True)
        yn.append((t * lax.rsqrt(ms + EPS) * gn_ref[:, gs]).astype(BF16))
    y_a = _dot(jnp.concatenate(yn, axis=1), wso_ref[...])

    pb = _pool_mix(pext_ref, wpg_ref, psc_ref, tile, i)
    y_b = _dot(pb, wpo_ref[...])

    sg = jax.nn.sigmoid(gt_ref[...])
    merged = (sg[:, :d] * y_a + sg[:, d:] * y_b).astype(BF16)
    h_ref[...] = x + mod_ref[2] * _dot(merged, wo_ref[...])


def _mixer(x, mod, g1, w_in_t, conv_w, conv_b, dt_bias, a_log, d_skip, g_norm,
           wso, wpg, psc, wpo, wo, heads, tile):
    s, d = x.shape
    inner = heads * SSD_HEAD_DIM
    width = conv_w.shape[1]
    pw = psc.shape[1]
    kern = functools.partial(_mixer_kernel, tile=tile, heads=heads)
    hbm = pl.BlockSpec(memory_space=pl.ANY)
    small = lambda a: _resident(a.shape)
    return pl.pallas_call(
        kern,
        grid=(s // tile,),
        in_specs=[_rows(tile, d), small(mod), small(g1), hbm,
                  small(conv_w), small(conv_b), small(dt_bias), small(a_log), small(d_skip), small(g_norm),
                  hbm, small(wpg), small(psc), hbm, hbm],
        out_specs=_rows(tile, d),
        out_shape=jax.ShapeDtypeStruct((s, d), F32),
        scratch_shapes=[pltpu.VMEM((SUBLANES, width), F32),
                        pltpu.VMEM((tile, width), F32),
                        pltpu.VMEM((SSD_STATE, inner), F32),
                        pltpu.VMEM((tile, inner), F32),
                        pltpu.VMEM((tile + POOL_HALO, pw), F32),
                        pltpu.VMEM((tile, inner), F32),
                        pltpu.VMEM((tile, 2 * d), F32),
                        pltpu.VMEM((inner + width, d), BF16),
                        pltpu.VMEM((d, LANES), BF16),
                        pltpu.VMEM((pw, d), BF16),
                        pltpu.VMEM((2 * d, d), BF16),
                        pltpu.VMEM(wso.shape, BF16),
                        pltpu.VMEM(wpo.shape, BF16),
                        pltpu.VMEM(wo.shape, BF16)],
        compiler_params=_params(),
        name="mixer",
    )(x, mod, g1, w_in_t, conv_w, conv_b, dt_bias, a_log, d_skip, g_norm, wso, wpg, psc, wpo, wo)


def _ffn_kernel(h_ref, mod_ref, g2_ref, win_hbm, cw_ref, cb_ref, wout_hbm, gf_ref,
                o_ref, gpad_ref, act_ref, win_ref, wout_ref, *, tile):
    f = wout_ref.shape[0]
    blk = FFN_COLS

    @pl.when(pl.program_id(0) == 0)
    def _():
        gpad_ref[0:SUBLANES, :] = jnp.zeros((SUBLANES, f), F32)
        in_rows = WEIGHT_ROWS // 2
        _stream_cast(win_hbm, in_rows, _cast_into(win_ref, in_rows))
        out_rows = f // 8
        _stream_cast(wout_hbm, out_rows, _cast_into(wout_ref, out_rows))

    h = h_ref[...]
    v = _modulated_rmsnorm(h, g2_ref[...], mod_ref[3], mod_ref[4]).astype(BF16)
    blocks = [slice(c0, min(c0 + blk, f)) for c0 in range(0, f, blk)]

    def gate_proj(cs):
        gpad_ref[SUBLANES:SUBLANES + tile, cs] = _dot(v, win_ref[:, f + cs.start:f + cs.stop])

    def conv_glu(cs):
        gb = gpad_ref[0:tile + SUBLANES, cs]
        acc = cb_ref[:, cs] + cw_ref[FFN_CONV - 1:FFN_CONV, cs] * gb[SUBLANES:]
        for shift in range(1, FFN_CONV):
            k = FFN_CONV - 1 - shift
            acc = acc + cw_ref[k:k + 1, cs] * pltpu.roll(gb, shift, axis=0)[SUBLANES:]
        gpad_ref[0:SUBLANES, cs] = gpad_ref[tile:tile + SUBLANES, cs]
        act_ref[:, cs] = (_silu(acc) * _dot(v, win_ref[:, cs])).astype(BF16)

    for j, cs in enumerate(blocks):
        gate_proj(cs)
        if j > 0:
            conv_glu(blocks[j - 1])
    conv_glu(blocks[-1])
    h2 = h + mod_ref[5] * _dot(act_ref[...], wout_ref[...])
    o_ref[...] = h2 * lax.rsqrt(jnp.mean(h2 * h2, axis=-1, keepdims=True) + EPS) * gf_ref[...]


def _ffn(h, mod, g2, win, conv_w, conv_b, wout, gf, tile):
    s, d = h.shape
    f = wout.shape[0]
    kern = functools.partial(_ffn_kernel, tile=tile)
    hbm = pl.BlockSpec(memory_space=pl.ANY)
    small = lambda a: _resident(a.shape)
    return pl.pallas_call(
        kern,
        grid=(s // tile,),
        in_specs=[_rows(tile, d), small(mod), small(g2), hbm, small(conv_w), small(conv_b), hbm, small(gf)],
        out_specs=_rows(tile, d),
        out_shape=jax.ShapeDtypeStruct((s, d), F32),
        scratch_shapes=[pltpu.VMEM((tile + SUBLANES, f), F32),
                        pltpu.VMEM((tile, f), BF16),
                        pltpu.VMEM(win.shape, BF16),
                        pltpu.VMEM(wout.shape, BF16)],
        compiler_params=_params(),
        name="ffn",
    )(h, mod, g2, win, conv_w, conv_b, wout, gf)


def _layer(h, c, w_ada, b_ada, g_norm1, w_in, ssd_conv_w, ssd_conv_b, ssd_dt_bias, ssd_a_log,
           ssd_d, g_ssd_norm, w_ssd_out, w_pool_grp, pool_scale, w_pool_out, w_out, g_norm2,
           w_ffn_in, ffn_conv_w, ffn_conv_b, w_ffn_out, g_final):
    s, d = h.shape
    heads = ssd_dt_bias.shape[0]
    inner = heads * SSD_HEAD_DIM
    conv_dim = ssd_conv_w.shape[1]
    pool_w = pool_scale.shape[0]
    f = ffn_conv_b.shape[0]
    mod = _ada(c, w_ada, b_ada)

    h1 = _mixer(h, mod, g_norm1.reshape(1, d), w_in.T,
                ssd_conv_w, ssd_conv_b.reshape(1, conv_dim),
                ssd_dt_bias.reshape(1, heads), ssd_a_log.reshape(1, heads),
                jnp.repeat(ssd_d, SSD_HEAD_DIM).reshape(1, inner), g_ssd_norm.reshape(1, inner),
                w_ssd_out, w_pool_grp.astype(BF16), pool_scale.reshape(1, pool_w), w_pool_out, w_out,
                heads, tile=MIXER_TILE)
    return _ffn(h1, mod, g_norm2.reshape(1, d), w_ffn_in, ffn_conv_w, ffn_conv_b.reshape(1, f),
                w_ffn_out, g_final.reshape(1, d), tile=FFN_TILE)


def kernel(x, c, w_ada, b_ada, g_norm1, w_in, ssd_conv_w, ssd_conv_b, ssd_dt_bias, ssd_a_log, ssd_d,
           g_ssd_norm, w_ssd_out, w_pool_grp, pool_scale, w_pool_out, w_out, g_norm2, w_ffn_in,
           ffn_conv_w, ffn_conv_b, w_ffn_out, g_final):
    assert x.shape[0] == 1 and w_ada.shape[0] == 1, "one sequence, one layer"
    out = _layer(x[0], c, w_ada[0], b_ada[0], g_norm1[0], w_in[0], ssd_conv_w[0], ssd_conv_b[0],
                 ssd_dt_bias[0], ssd_a_log[0], ssd_d[0], g_ssd_norm[0], w_ssd_out[0], w_pool_grp[0],
                 pool_scale[0], w_pool_out[0], w_out[0], g_norm2[0], w_ffn_in[0], ffn_conv_w[0],
                 ffn_conv_b[0], w_ffn_out[0], g_final)
    return out[None]
```

```python
import functools

import jax
import jax.numpy as jnp
from jax import lax
from jax.experimental import pallas as pl
from jax.experimental.pallas import tpu as pltpu

F32 = jnp.float32
BF16 = jnp.bfloat16

EPS = 1e-6
LOG2E = 1.4426950408889634
SSD_HEAD_DIM = 64
SSD_GROUPS = 4
SSD_STATE = 128
SSD_CONV = 4
SSD_CHUNK = 128
POOL_WINDOWS = (2, 4, 8, 16)
POOL_HALO = 16
FFN_CONV = 3
LANES = 128
SUBLANES = 8
VMEM_LIMIT = 56 * 1024 * 1024
MIXER_TILE = 256
FFN_TILE = 512
MIXER_COLS = 4 * LANES
FFN_COLS = 2 * LANES
WEIGHT_ROWS = 128
STAGE_SLOTS = 8


def _dot(a, b):
    return jnp.dot(a, b, preferred_element_type=F32)


def _dot_nt(a, b):
    return lax.dot_general(a, b, (((1,), (1,)), ((), ())), preferred_element_type=F32)


def _silu(v):
    return v * jax.nn.sigmoid(v)


def _resident(shape):
    return pl.BlockSpec(shape, lambda i: (0,) * len(shape), pipeline_mode=pl.Buffered(1))


def _rows(tile, width):
    return pl.BlockSpec((tile, width), lambda i: (i, 0))


def _params(vmem=VMEM_LIMIT):
    return pltpu.CompilerParams(dimension_semantics=("arbitrary",), vmem_limit_bytes=vmem)


def _stream_cast(src_hbm, row_chunk, store, row0=0, rows=None):
    rows = src_hbm.shape[0] - row0 if rows is None else rows
    width = src_hbm.shape[1]
    n = rows // row_chunk
    assert n * row_chunk == rows

    def body(stage, sem):
        def copy(k):
            slot = k % STAGE_SLOTS
            src = src_hbm.at[pl.ds(row0 + k * row_chunk, row_chunk), :]
            return pltpu.make_async_copy(src, stage.at[slot], sem.at[slot])
        for k in range(min(STAGE_SLOTS - 1, n)):
            copy(k).start()
        for k in range(n):
            if k + STAGE_SLOTS - 1 < n:
                copy(k + STAGE_SLOTS - 1).start()
            copy(k).wait()
            store(k, stage[k % STAGE_SLOTS])

    pl.run_scoped(body, pltpu.VMEM((STAGE_SLOTS, row_chunk, width), F32),
                  pltpu.SemaphoreType.DMA((STAGE_SLOTS,)))


def _cast_into(dst_ref, row_chunk):
    def store(k, chunk):
        dst_ref[k * row_chunk:(k + 1) * row_chunk, :] = chunk.astype(dst_ref.dtype)
    return store


def _ada_kernel(c_ref, w_hbm, b_ref, o_ref, acc_ref):
    d = c_ref.shape[1]
    cond = _silu(c_ref[...])
    cond8 = jnp.broadcast_to(cond, (SUBLANES, d)).astype(BF16)
    acc_ref[...] = jnp.zeros(acc_ref.shape, F32)

    def store(k, chunk):
        ks = slice(k * WEIGHT_ROWS, (k + 1) * WEIGHT_ROWS)
        acc_ref[...] += _dot(cond8[:, ks], chunk.astype(BF16))
    _stream_cast(w_hbm, WEIGHT_ROWS, store)
    for j in range(o_ref.shape[0]):
        o_ref[j] = acc_ref[0:1, j * d:(j + 1) * d] + b_ref[:, j * d:(j + 1) * d]


def _ada(c, w_ada, b_ada):
    d, n = w_ada.shape
    return pl.pallas_call(
        _ada_kernel,
        grid=(1,),
        in_specs=[_resident((1, d)), pl.BlockSpec(memory_space=pl.ANY), _resident((1, n))],
        out_specs=pl.BlockSpec((n // d, 1, d), lambda i: (0, 0, 0)),
        out_shape=jax.ShapeDtypeStruct((n // d, 1, d), F32),
        scratch_shapes=[pltpu.VMEM((SUBLANES, n), F32)],
        compiler_params=_params(),
        name="ada",
    )(c, w_ada, b_ada.reshape(1, n))


def _modulated_rmsnorm(x, g, shift, scale):
    y = x * lax.rsqrt(jnp.mean(x * x, axis=-1, keepdims=True) + EPS) * g
    return y * (1.0 + scale) + shift


def _split3(v):
    hi = v.astype(BF16)
    r = v - hi.astype(F32)
    mid = r.astype(BF16)
    lo = (r - mid.astype(F32)).astype(BF16)
    return hi, mid, lo


def _conv_silu_block(xblk, halo_ref, xc_ref, cw_ref, cb_ref, cs):
    rows = xblk.shape[0]
    xb = jnp.concatenate([halo_ref[:, cs], xblk], axis=0)
    acc = cb_ref[:, cs] + cw_ref[SSD_CONV - 1:SSD_CONV, cs] * xblk
    for shift in range(1, SSD_CONV):
        k = SSD_CONV - 1 - shift
        acc = acc + cw_ref[k:k + 1, cs] * pltpu.roll(xb, shift, axis=0)[SUBLANES:]
    xc_ref[:, cs] = _silu(acc)
    halo_ref[:, cs] = xblk[rows - SUBLANES:]


def _ssd_chunk(xc_ref, r0, dtv, alog, dsk_ref, st_ref, y_ref, fillers, *, inner, heads):
    L = SSD_CHUNK
    N = SSD_STATE
    P = SSD_HEAD_DIM
    hpg = heads // SSD_GROUPS
    rs = slice(r0, r0 + L)

    da = dtv * (-jnp.exp(alog))
    row = lax.broadcasted_iota(jnp.int32, (L, L), 0)
    col = lax.broadcasted_iota(jnp.int32, (L, L), 1)
    causal = row >= col
    tril = jnp.where(causal, 1.0, 0.0).astype(BF16)
    hi, mid, lo = _split3(da)
    acum = (_dot(tril, hi) + _dot(tril, mid) + _dot(tril, lo)) * LOG2E
    src_t = (acum - jnp.log2(dtv)).T
    last = acum[L - 1:L, :]
    wgt = dtv * jnp.exp2(last - acum)
    ea_last = jnp.exp2(last)

    lane = lax.broadcasted_iota(jnp.int32, (L, 2 * LANES), 1)
    lane1 = lax.broadcasted_iota(jnp.int32, (L, LANES), 1)
    lane_row = lax.broadcasted_iota(jnp.int32, (1, LANES), 1)

    for g in range(SSD_GROUPS):
        b_g = xc_ref[rs, inner + g * N: inner + (g + 1) * N]
        c_g = xc_ref[rs, inner + (SSD_GROUPS + g) * N: inner + (SSD_GROUPS + g + 1) * N]
        cb = lax.dot_general(c_g.astype(BF16), b_g.astype(BF16),
                             (((1,), (1,)), ((), ())), preferred_element_type=F32)
        scale_rows = []
        for quad in range(hpg // 4):
            h0 = g * hpg + quad * 4
            q0 = h0 * P
            lhs = []
            for h in range(h0, h0 + 4):
                colb = jnp.broadcast_to(acum[:, h:h + 1], (L, L))
                w_h = cb * jnp.exp2(jnp.where(causal, colb - src_t[h:h + 1, :], -jnp.inf))
                e_h = c_g * jnp.exp2(colb)
                lhs.append(jnp.concatenate([w_h.astype(BF16), e_h.astype(BF16)], axis=1))
                scale_rows.append(jnp.broadcast_to(ea_last[:, h:h + 1], (1, LANES)))
            lhs = jnp.concatenate(lhs, axis=0)
            xq = xc_ref[rs, q0:q0 + 4 * P]
            rhs = jnp.concatenate([xq.astype(BF16), st_ref[:, q0:q0 + 4 * P].astype(BF16)], axis=0)
            out = _dot(lhs, rhs)
            yq = jnp.where(lane < P, out[0:L],
                           jnp.where(lane < 2 * P, out[L:2 * L],
                                     jnp.where(lane < 3 * P, out[2 * L:3 * L], out[3 * L:4 * L])))
            y_ref[rs, q0:q0 + 4 * P] = yq + dsk_ref[:, q0:q0 + 4 * P] * xq
            next(fillers, lambda: None)()
        xw = []
        for pair in range(hpg // 2):
            h0 = g * hpg + pair * 2
            p0 = h0 * P
            wb = jnp.where(lane1 < P,
                           jnp.broadcast_to(wgt[:, h0:h0 + 1], (L, LANES)),
                           jnp.broadcast_to(wgt[:, h0 + 1:h0 + 2], (L, LANES)))
            xw.append((xc_ref[rs, p0:p0 + 2 * P] * wb).astype(BF16))
            srow = jnp.where(lane_row < P, scale_rows[pair * 2], scale_rows[pair * 2 + 1])
            st_ref[:, p0:p0 + 2 * P] = st_ref[:, p0:p0 + 2 * P] * srow
        xw = jnp.concatenate(xw, axis=1)
        g0 = g * hpg * P
        st_ref[:, g0:g0 + hpg * P] += _dot(b_g.T.astype(BF16), xw)


def _pool_mix(pext_ref, wpg_ref, psc_ref, tile, step_idx):
    pw = pext_ref.shape[1]
    gd = pw // len(POOL_WINDOWS)
    t_glob = step_idx * tile + lax.broadcasted_iota(jnp.int32, (tile, 1), 0) + 1
    pooled = []
    for g, w in enumerate(POOL_WINDOWS):
        gs = slice(g * gd, (g + 1) * gd)
        acc = pext_ref[:, gs]
        step = 1
        while step < w:
            acc = acc + pltpu.roll(acc, step, axis=0)
            step *= 2
        inv_cnt = 1.0 / jnp.minimum(t_glob, w).astype(F32)
        pg = acc[POOL_HALO:, :] * inv_cnt - pext_ref[POOL_HALO:POOL_HALO + tile, gs]
        pooled.append(_dot(pg.astype(BF16), wpg_ref[g]))
    pext_ref[0:POOL_HALO, :] = pext_ref[tile:tile + POOL_HALO, :]
    return (jnp.concatenate(pooled, axis=1) * psc_ref[...]).astype(BF16)


def _mixer_kernel(x_ref, mod_ref, g1_ref, w_in_t_hbm,
                  cw_ref, cb_ref, dtb_ref, alog_ref, dsk_ref, gn_ref,
                  wso_hbm, wpg_ref, psc_ref, wpo_hbm, wo_hbm,
                  h_ref, halo_ref, xc_ref, st_ref, y_ref, pext_ref, z_ref, gt_ref,
                  wzx_ref, wdt_ref, wp_ref, wg_ref, wso_ref, wpo_ref, wo_ref, *, tile, heads):
    d = x_ref.shape[1]
    inner = y_ref.shape[1]
    L = SSD_CHUNK
    i = pl.program_id(0)

    @pl.when(i == 0)
    def _():
        halo_ref[...] = jnp.zeros(halo_ref.shape, F32)
        st_ref[...] = jnp.zeros(st_ref.shape, F32)
        pext_ref[0:POOL_HALO, :] = jnp.zeros((POOL_HALO, pext_ref.shape[1]), F32)
        chunk_rows = 2 * WEIGHT_ROWS
        n_zx, pw, gw2 = wzx_ref.shape[0], wp_ref.shape[0], wg_ref.shape[0]
        _stream_cast(w_in_t_hbm, chunk_rows, _cast_into(wzx_ref, chunk_rows), 0, n_zx)

        def store_dt(k, chunk):
            lane = lax.broadcasted_iota(jnp.int32, (chunk.shape[1], LANES), 1)
            wdt_ref[...] = jnp.where(lane < heads, chunk.T, 0.0).astype(BF16)
        _stream_cast(w_in_t_hbm, LANES, store_dt, n_zx, LANES)
        _stream_cast(w_in_t_hbm, chunk_rows, _cast_into(wp_ref, chunk_rows), n_zx + heads, pw)
        _stream_cast(w_in_t_hbm, chunk_rows, _cast_into(wg_ref, chunk_rows), n_zx + heads + pw, gw2)
        for src, dst in ((wso_hbm, wso_ref), (wpo_hbm, wpo_ref), (wo_hbm, wo_ref)):
            _stream_cast(src, chunk_rows, _cast_into(dst, chunk_rows))

    x = x_ref[...]
    u = _modulated_rmsnorm(x, g1_ref[...], mod_ref[0], mod_ref[1]).astype(BF16)

    blk = MIXER_COLS

    def proj(dst_ref, r_off, w_ref, w_off, cs):
        def run():
            dst_ref[r_off:r_off + tile, cs] = _dot_nt(u, w_ref[w_off + cs.start:w_off + cs.stop, :])
        return run

    for c0 in range(0, xc_ref.shape[1], blk):
        xblk = _dot_nt(u, wzx_ref[inner + c0:inner + c0 + blk, :])
        _conv_silu_block(xblk, halo_ref, xc_ref, cw_ref, cb_ref, slice(c0, c0 + blk))
    pad = jnp.zeros((1, LANES - heads), F32)
    dtb = jnp.concatenate([dtb_ref[...], pad], axis=1)
    alog = jnp.concatenate([alog_ref[...], pad], axis=1)
    dtv = jax.nn.softplus(_dot(u, wdt_ref[...]) + dtb)

    fillers = iter([proj(z_ref, 0, wzx_ref, 0, slice(c0, c0 + blk)) for c0 in range(0, inner, blk)]
                   + [proj(pext_ref, POOL_HALO, wp_ref, 0, slice(c0, c0 + blk))
                      for c0 in range(0, pext_ref.shape[1], blk)]
                   + [proj(gt_ref, 0, wg_ref, 0, slice(c0, c0 + blk)) for c0 in range(0, 2 * d, blk)])
    for c in range(tile // L):
        _ssd_chunk(xc_ref, c * L, dtv[c * L:(c + 1) * L], alog, dsk_ref, st_ref, y_ref, fillers,
                   inner=inner, heads=heads)
    for run in fillers:
        run()

    gw = inner // SSD_GROUPS
    yn = []
    for g in range(SSD_GROUPS):
        gs = slice(g * gw, (g + 1) * gw)
        t = y_ref[:, gs] * _silu(z_ref[:, gs])
        ms = jnp.mean(t * t, axis=-1, keepdims=True)
        yn.append((t * lax.rsqrt(ms + EPS) * gn_ref[:, gs]).astype(BF16))
    y_a = _dot(jnp.concatenate(yn, axis=1), wso_ref[...])

    pb = _pool_mix(pext_ref, wpg_ref, psc_ref, tile, i)
    y_b = _dot(pb, wpo_ref[...])

    sg = jax.nn.sigmoid(gt_ref[...])
    merged = (sg[:, :d] * y_a + sg[:, d:] * y_b).astype(BF16)
    h_ref[...] = x + mod_ref[2] * _dot(merged, wo_ref[...])


def _mixer(x, mod, g1, w_in_t, conv_w, conv_b, dt_bias, a_log, d_skip, g_norm,
           wso, wpg, psc, wpo, wo, heads, tile):
    s, d = x.shape
    inner = heads * SSD_HEAD_DIM
    width = conv_w.shape[1]
    pw = psc.shape[1]
    kern = functools.partial(_mixer_kernel, tile=tile, heads=heads)
    hbm = pl.BlockSpec(memory_space=pl.ANY)
    small = lambda a: _resident(a.shape)
    return pl.pallas_call(
        kern,
        grid=(s // tile,),
        in_specs=[_rows(tile, d), small(mod), small(g1), hbm,
                  small(conv_w), small(conv_b), small(dt_bias), small(a_log), small(d_skip), small(g_norm),
                  hbm, small(wpg), small(psc), hbm, hbm],
        out_specs=_rows(tile, d),
        out_shape=jax.ShapeDtypeStruct((s, d), F32),
        scratch_shapes=[pltpu.VMEM((SUBLANES, width), F32),
                        pltpu.VMEM((tile, width), F32),
                        pltpu.VMEM((SSD_STATE, inner), F32),
                        pltpu.VMEM((tile, inner), F32),
                        pltpu.VMEM((tile + POOL_HALO, pw), F32),
                        pltpu.VMEM((tile, inner), F32),
                        pltpu.VMEM((tile, 2 * d), F32),
                        pltpu.VMEM((inner + width, d), BF16),
                        pltpu.VMEM((d, LANES), BF16),
                        pltpu.VMEM((pw, d), BF16),
                        pltpu.VMEM((2 * d, d), BF16),
                        pltpu.VMEM(wso.shape, BF16),
                        pltpu.VMEM(wpo.shape, BF16),
                        pltpu.VMEM(wo.shape, BF16)],
        compiler_params=_params(),
        name="mixer",
    )(x, mod, g1, w_in_t, conv_w, conv_b, dt_bias, a_log, d_skip, g_norm, wso, wpg, psc, wpo, wo)


def _ffn_kernel(h_ref, mod_ref, g2_ref, win_hbm, cw_ref, cb_ref, wout_hbm, gf_ref,
                o_ref, gpad_ref, act_ref, win_ref, wout_ref, *, tile):
    f = wout_ref.shape[0]
    blk = FFN_COLS

    @pl.when(pl.program_id(0) == 0)
    def _():
        gpad_ref[0:SUBLANES, :] = jnp.zeros((SUBLANES, f), F32)
        in_rows = WEIGHT_ROWS // 2
        _stream_cast(win_hbm, in_rows, _cast_into(win_ref, in_rows))
        out_rows = f // 8
        _stream_cast(wout_hbm, out_rows, _cast_into(wout_ref, out_rows))

    h = h_ref[...]
    v = _modulated_rmsnorm(h, g2_ref[...], mod_ref[3], mod_ref[4]).astype(BF16)
    blocks = [slice(c0, min(c0 + blk, f)) for c0 in range(0, f, blk)]

    def gate_proj(cs):
        gpad_ref[SUBLANES:SUBLANES + tile, cs] = _dot(v, win_ref[:, f + cs.start:f + cs.stop])

    def conv_glu(cs):
        gb = gpad_ref[0:tile + SUBLANES, cs]
        acc = cb_ref[:, cs] + cw_ref[FFN_CONV - 1:FFN_CONV, cs] * gb[SUBLANES:]
        for shift in range(1, FFN_CONV):
            k = FFN_CONV - 1 - shift
            acc = acc + cw_ref[k:k + 1, cs] * pltpu.roll(gb, shift, axis=0)[SUBLANES:]
        gpad_ref[0:SUBLANES, cs] = gpad_ref[tile:tile + SUBLANES, cs]
        act_ref[:, cs] = (_silu(acc) * _dot(v, win_ref[:, cs])).astype(BF16)

    for j, cs in enumerate(blocks):
        gate_proj(cs)
        if j > 0:
            conv_glu(blocks[j - 1])
    conv_glu(blocks[-1])
    h2 = h + mod_ref[5] * _dot(act_ref[...], wout_ref[...])
    o_ref[...] = h2 * lax.rsqrt(jnp.mean(h2 * h2, axis=-1, keepdims=True) + EPS) * gf_ref[...]


def _ffn(h, mod, g2, win, conv_w, conv_b, wout, gf, tile):
    s, d = h.shape
    f = wout.shape[0]
    kern = functools.partial(_ffn_kernel, tile=tile)
    hbm = pl.BlockSpec(memory_space=pl.ANY)
    small = lambda a: _resident(a.shape)
    return pl.pallas_call(
        kern,
        grid=(s // tile,),
        in_specs=[_rows(tile, d), small(mod), small(g2), hbm, small(conv_w), small(conv_b), hbm, small(gf)],
        out_specs=_rows(tile, d),
        out_shape=jax.ShapeDtypeStruct((s, d), F32),
        scratch_shapes=[pltpu.VMEM((tile + SUBLANES, f), F32),
                        pltpu.VMEM((tile, f), BF16),
                        pltpu.VMEM(win.shape, BF16),
                        pltpu.VMEM(wout.shape, BF16)],
        compiler_params=_params(),
        name="ffn",
    )(h, mod, g2, win, conv_w, conv_b, wout, gf)


def _layer(h, c, w_ada, b_ada, g_norm1, w_in, ssd_conv_w, ssd_conv_b, ssd_dt_bias, ssd_a_log,
           ssd_d, g_ssd_norm, w_ssd_out, w_pool_grp, pool_scale, w_pool_out, w_out, g_norm2,
           w_ffn_in, ffn_conv_w, ffn_conv_b, w_ffn_out, g_final):
    s, d = h.shape
    heads = ssd_dt_bias.shape[0]
    inner = heads * SSD_HEAD_DIM
    conv_dim = ssd_conv_w.shape[1]
    pool_w = pool_scale.shape[0]
    f = ffn_conv_b.shape[0]
    mod = _ada(c, w_ada, b_ada)

    h1 = _mixer(h, mod, g_norm1.reshape(1, d), w_in.T,
                ssd_conv_w, ssd_conv_b.reshape(1, conv_dim),
                ssd_dt_bias.reshape(1, heads), ssd_a_log.reshape(1, heads),
                jnp.repeat(ssd_d, SSD_HEAD_DIM).reshape(1, inner), g_ssd_norm.reshape(1, inner),
                w_ssd_out, w_pool_grp.astype(BF16), pool_scale.reshape(1, pool_w), w_pool_out, w_out,
                heads, tile=MIXER_TILE)
    return _ffn(h1, mod, g_norm2.reshape(1, d), w_ffn_in, ffn_conv_w, ffn_conv_b.reshape(1, f),
                w_ffn_out, g_final.reshape(1, d), tile=FFN_TILE)


def kernel(x, c, w_ada, b_ada, g_norm1, w_in, ssd_conv_w, ssd_conv_b, ssd_dt_bias, ssd_a_log, ssd_d,
           g_ssd_norm, w_ssd_out, w_pool_grp, pool_scale, w_pool_out, w_out, g_norm2, w_ffn_in,
           ffn_conv_w, ffn_conv_b, w_ffn_out, g_final):
    assert x.shape[0] == 1 and w_ada.shape[0] == 1, "one sequence, one layer"
    out = _layer(x[0], c, w_ada[0], b_ada[0], g_norm1[0], w_in[0], ssd_conv_w[0], ssd_conv_b[0],
                 ssd_dt_bias[0], ssd_a_log[0], ssd_d[0], g_ssd_norm[0], w_ssd_out[0], w_pool_grp[0],
                 pool_scale[0], w_pool_out[0], w_out[0], g_norm2[0], w_ffn_in[0], ffn_conv_w[0],
                 ffn_conv_b[0], w_ffn_out[0], g_final)
    return out[None]
```

```python
import functools

import jax
import jax.numpy as jnp
from jax import lax
from jax.experimental import pallas as pl
from jax.experimental.pallas import tpu as pltpu

F32 = jnp.float32
BF16 = jnp.bfloat16

EPS = 1e-6
LOG2E = 1.4426950408889634
SSD_HEAD_DIM = 64
SSD_GROUPS = 4
SSD_STATE = 128
SSD_CONV = 4
SSD_CHUNK = 128
POOL_WINDOWS = (2, 4, 8, 16)
POOL_HALO = 16
FFN_CONV = 3
LANES = 128
SUBLANES = 8
VMEM_LIMIT = 56 * 1024 * 1024
MIXER_TILE = 256
FFN_TILE = 512
MIXER_COLS = 4 * LANES
FFN_COLS = 2 * LANES
WEIGHT_ROWS = 128
STAGE_SLOTS = 8


def _dot(a, b):
    return jnp.dot(a, b, preferred_element_type=F32)


def _dot_nt(a, b):
    return lax.dot_general(a, b, (((1,), (1,)), ((), ())), preferred_element_type=F32)


def _silu(v):
    return v * jax.nn.sigmoid(v)


def _resident(shape):
    return pl.BlockSpec(shape, lambda i: (0,) * len(shape), pipeline_mode=pl.Buffered(1))


def _rows(tile, width):
    return pl.BlockSpec((tile, width), lambda i: (i, 0))


def _params(vmem=VMEM_LIMIT):
    return pltpu.CompilerParams(dimension_semantics=("arbitrary",), vmem_limit_bytes=vmem)


def _stream_cast(src_hbm, row_chunk, store, row0=0, rows=None):
    rows = src_hbm.shape[0] - row0 if rows is None else rows
    width = src_hbm.shape[1]
    n = rows // row_chunk
    assert n * row_chunk == rows

    def body(stage, sem):
        def copy(k):
            slot = k % STAGE_SLOTS
            src = src_hbm.at[pl.ds(row0 + k * row_chunk, row_chunk), :]
            return pltpu.make_async_copy(src, stage.at[slot], sem.at[slot])
        for k in range(min(STAGE_SLOTS - 1, n)):
            copy(k).start()
        for k in range(n):
            if k + STAGE_SLOTS - 1 < n:
                copy(k + STAGE_SLOTS - 1).start()
            copy(k).wait()
            store(k, stage[k % STAGE_SLOTS])

    pl.run_scoped(body, pltpu.VMEM((STAGE_SLOTS, row_chunk, width), F32),
                  pltpu.SemaphoreType.DMA((STAGE_SLOTS,)))


def _cast_into(dst_ref, row_chunk):
    def store(k, chunk):
        dst_ref[k * row_chunk:(k + 1) * row_chunk, :] = chunk.astype(dst_ref.dtype)
    return store


def _ada_kernel(c_ref, w_hbm, b_ref, o_ref, acc_ref):
    d = c_ref.shape[1]
    cond = _silu(c_ref[...])
    cond8 = jnp.broadcast_to(cond, (SUBLANES, d)).astype(BF16)
    acc_ref[...] = jnp.zeros(acc_ref.shape, F32)

    def store(k, chunk):
        ks = slice(k * WEIGHT_ROWS, (k + 1) * WEIGHT_ROWS)
        acc_ref[...] += _dot(cond8[:, ks], chunk.astype(BF16))
    _stream_cast(w_hbm, WEIGHT_ROWS, store)
    for j in range(o_ref.shape[0]):
        o_ref[j] = acc_ref[0:1, j * d:(j + 1) * d] + b_ref[:, j * d:(j + 1) * d]


def _ada(c, w_ada, b_ada):
    d, n = w_ada.shape
    return pl.pallas_call(
        _ada_kernel,
        grid=(1,),
        in_specs=[_resident((1, d)), pl.BlockSpec(memory_space=pl.ANY), _resident((1, n))],
        out_specs=pl.BlockSpec((n // d, 1, d), lambda i: (0, 0, 0)),
        out_shape=jax.ShapeDtypeStruct((n // d, 1, d), F32),
        scratch_shapes=[pltpu.VMEM((SUBLANES, n), F32)],
        compiler_params=_params(),
        name="ada",
    )(c, w_ada, b_ada.reshape(1, n))


def _modulated_rmsnorm(x, g, shift, scale):
    y = x * lax.rsqrt(jnp.mean(x * x, axis=-1, keepdims=True) + EPS) * g
    return y * (1.0 + scale) + shift


def _split3(v):
    hi = v.astype(BF16)
    r = v - hi.astype(F32)
    mid = r.astype(BF16)
    lo = (r - mid.astype(F32)).astype(BF16)
    return hi, mid, lo


def _conv_silu_block(xblk, halo_ref, xc_ref, cw_ref, cb_ref, cs):
    rows = xblk.shape[0]
    xb = jnp.concatenate([halo_ref[:, cs], xblk], axis=0)
    acc = cb_ref[:, cs] + cw_ref[SSD_CONV - 1:SSD_CONV, cs] * xblk
    for shift in range(1, SSD_CONV):
        k = SSD_CONV - 1 - shift
        acc = acc + cw_ref[k:k + 1, cs] * pltpu.roll(xb, shift, axis=0)[SUBLANES:]
    xc_ref[:, cs] = _silu(acc)
    halo_ref[:, cs] = xblk[rows - SUBLANES:]


def _ssd_chunk(xc_ref, r0, dtv, alog, dsk_ref, st_ref, y_ref, fillers, *, inner, heads):
    L = SSD_CHUNK
    N = SSD_STATE
    P = SSD_HEAD_DIM
    hpg = heads // SSD_GROUPS
    rs = slice(r0, r0 + L)

    da = dtv * (-jnp.exp(alog))
    row = lax.broadcasted_iota(jnp.int32, (L, L), 0)
    col = lax.broadcasted_iota(jnp.int32, (L, L), 1)
    causal = row >= col
    tril = jnp.where(causal, 1.0, 0.0).astype(BF16)
    hi, mid, lo = _split3(da)
    acum = (_dot(tril, hi) + _dot(tril, mid) + _dot(tril, lo)) * LOG2E
    src_t = (acum - jnp.log2(dtv)).T
    last = acum[L - 1:L, :]
    wgt = dtv * jnp.exp2(last - acum)
    ea_last = jnp.exp2(last)

    lane = lax.broadcasted_iota(jnp.int32, (L, 2 * LANES), 1)
    lane1 = lax.broadcasted_iota(jnp.int32, (L, LANES), 1)
    lane_row = lax.broadcasted_iota(jnp.int32, (1, LANES), 1)

    for g in range(SSD_GROUPS):
        b_g = xc_ref[rs, inner + g * N: inner + (g + 1) * N]
        c_g = xc_ref[rs, inner + (SSD_GROUPS + g) * N: inner + (SSD_GROUPS + g + 1) * N]
        c_b = c_g.astype(BF16)
        cb_b = lax.dot_general(c_b, b_g.astype(BF16),
                               (((1,), (1,)), ((), ())), preferred_element_type=F32).astype(BF16)
        scale_rows = []
        for quad in range(hpg // 4):
            h0 = g * hpg + quad * 4
            q0 = h0 * P
            lhs = []
            for h in range(h0, h0 + 4):
                colb = jnp.broadcast_to(acum[:, h:h + 1], (L, L))
                w_h = cb_b * jnp.exp2(jnp.where(causal, colb - src_t[h:h + 1, :], -jnp.inf)).astype(BF16)
                e_h = c_b * jnp.exp2(colb).astype(BF16)
                lhs.append(jnp.concatenate([w_h, e_h], axis=1))
                scale_rows.append(jnp.broadcast_to(ea_last[:, h:h + 1], (1, LANES)))
            lhs = jnp.concatenate(lhs, axis=0)
            xq = xc_ref[rs, q0:q0 + 4 * P]
            rhs = jnp.concatenate([xq.astype(BF16), st_ref[:, q0:q0 + 4 * P].astype(BF16)], axis=0)
            out = _dot(lhs, rhs)
            yq = jnp.where(lane < P, out[0:L],
                           jnp.where(lane < 2 * P, out[L:2 * L],
                                     jnp.where(lane < 3 * P, out[2 * L:3 * L], out[3 * L:4 * L])))
            y_ref[rs, q0:q0 + 4 * P] = yq + dsk_ref[:, q0:q0 + 4 * P] * xq
            next(fillers, lambda: None)()
        xw = []
        for pair in range(hpg // 2):
            h0 = g * hpg + pair * 2
            p0 = h0 * P
            wb = jnp.where(lane1 < P,
                           jnp.broadcast_to(wgt[:, h0:h0 + 1], (L, LANES)),
                           jnp.broadcast_to(wgt[:, h0 + 1:h0 + 2], (L, LANES)))
            xw.append((xc_ref[rs, p0:p0 + 2 * P] * wb).astype(BF16))
            srow = jnp.where(lane_row < P, scale_rows[pair * 2], scale_rows[pair * 2 + 1])
            st_ref[:, p0:p0 + 2 * P] = st_ref[:, p0:p0 + 2 * P] * srow
        xw = jnp.concatenate(xw, axis=1)
        g0 = g * hpg * P
        st_ref[:, g0:g0 + hpg * P] += _dot(b_g.T.astype(BF16), xw)


def _pool_mix(pext_ref, wpg_ref, psc_ref, tile, step_idx):
    pw = pext_ref.shape[1]
    gd = pw // len(POOL_WINDOWS)
    t_glob = step_idx * tile + lax.broadcasted_iota(jnp.int32, (tile, 1), 0) + 1
    pooled = []
    for g, w in enumerate(POOL_WINDOWS):
        gs = slice(g * gd, (g + 1) * gd)
        acc = pext_ref[:, gs]
        step = 1
        while step < w:
            acc = acc + pltpu.roll(acc, step, axis=0)
            step *= 2
        inv_cnt = 1.0 / jnp.minimum(t_glob, w).astype(F32)
        pg = acc[POOL_HALO:, :] * inv_cnt - pext_ref[POOL_HALO:POOL_HALO + tile, gs]
        pooled.append(_dot(pg.astype(BF16), wpg_ref[g]))
    pext_ref[0:POOL_HALO, :] = pext_ref[tile:tile + POOL_HALO, :]
    return (jnp.concatenate(pooled, axis=1) * psc_ref[...]).astype(BF16)


def _mixer_kernel(x_ref, mod_ref, g1_ref, w_in_t_hbm,
                  cw_ref, cb_ref, dtb_ref, alog_ref, dsk_ref, gn_ref,
                  wso_hbm, wpg_ref, psc_ref, wpo_hbm, wo_hbm,
                  h_ref, halo_ref, xc_ref, st_ref, y_ref, pext_ref, z_ref, gt_ref,
                  wzx_ref, wdt_ref, wp_ref, wg_ref, wso_ref, wpo_ref, wo_ref, *, tile, heads):
    d = x_ref.shape[1]
    inner = y_ref.shape[1]
    L = SSD_CHUNK
    i = pl.program_id(0)

    @pl.when(i == 0)
    def _():
        halo_ref[...] = jnp.zeros(halo_ref.shape, F32)
        st_ref[...] = jnp.zeros(st_ref.shape, F32)
        pext_ref[0:POOL_HALO, :] = jnp.zeros((POOL_HALO, pext_ref.shape[1]), F32)
        chunk_rows = 2 * WEIGHT_ROWS
        n_zx, pw, gw2 = wzx_ref.shape[0], wp_ref.shape[0], wg_ref.shape[0]
        _stream_cast(w_in_t_hbm, chunk_rows, _cast_into(wzx_ref, chunk_rows), 0, n_zx)

        def store_dt(k, chunk):
            lane = lax.broadcasted_iota(jnp.int32, (chunk.shape[1], LANES), 1)
            wdt_ref[...] = jnp.where(lane < heads, chunk.T, 0.0).astype(BF16)
        _stream_cast(w_in_t_hbm, LANES, store_dt, n_zx, LANES)
        _stream_cast(w_in_t_hbm, chunk_rows, _cast_into(wp_ref, chunk_rows), n_zx + heads, pw)
        _stream_cast(w_in_t_hbm, chunk_rows, _cast_into(wg_ref, chunk_rows), n_zx + heads + pw, gw2)
        for src, dst in ((wso_hbm, wso_ref), (wpo_hbm, wpo_ref), (wo_hbm, wo_ref)):
            _stream_cast(src, chunk_rows, _cast_into(dst, chunk_rows))

    x = x_ref[...]
    u = _modulated_rmsnorm(x, g1_ref[...], mod_ref[0], mod_ref[1]).astype(BF16)

    blk = MIXER_COLS

    def proj(dst_ref, r_off, w_ref, w_off, cs):
        def run():
            dst_ref[r_off:r_off + tile, cs] = _dot_nt(u, w_ref[w_off + cs.start:w_off + cs.stop, :])
        return run

    for c0 in range(0, xc_ref.shape[1], blk):
        xblk = _dot_nt(u, wzx_ref[inner + c0:inner + c0 + blk, :])
        _conv_silu_block(xblk, halo_ref, xc_ref, cw_ref, cb_ref, slice(c0, c0 + blk))
    pad = jnp.zeros((1, LANES - heads), F32)
    dtb = jnp.concatenate([dtb_ref[...], pad], axis=1)
    alog = jnp.concatenate([alog_ref[...], pad], axis=1)
    dtv = jax.nn.softplus(_dot(u, wdt_ref[...]) + dtb)

    fillers = iter([proj(z_ref, 0, wzx_ref, 0, slice(c0, c0 + blk)) for c0 in range(0, inner, blk)]
                   + [proj(pext_ref, POOL_HALO, wp_ref, 0, slice(c0, c0 + blk))
                      for c0 in range(0, pext_ref.shape[1], blk)]
                   + [proj(gt_ref, 0, wg_ref, 0, slice(c0, c0 + blk)) for c0 in range(0, 2 * d, blk)])
    for c in range(tile // L):
        _ssd_chunk(xc_ref, c * L, dtv[c * L:(c + 1) * L], alog, dsk_ref, st_ref, y_ref, fillers,
                   inner=inner, heads=heads)
    for run in fillers:
        run()

    gw = inner // SSD_GROUPS
    yn = []
    for g in range(SSD_GROUPS):
        gs = slice(g * gw, (g + 1) * gw)
        t = y_ref[:, gs] * _silu(z_ref[:, gs])
        ms = jnp.mean(t * t, axis=-1, keepdims=True)
        yn.append((t * lax.rsqrt(ms + EPS) * gn_ref[:, gs]).astype(BF16))
    y_a = _dot(jnp.concatenate(yn, axis=1), wso_ref[...])

    pb = _pool_mix(pext_ref, wpg_ref, psc_ref, tile, i)
    y_b = _dot(pb, wpo_ref[...])

    sg = jax.nn.sigmoid(gt_ref[...])
    merged = (sg[:, :d] * y_a + sg[:, d:] * y_b).astype(BF16)
    h_ref[...] = x + mod_ref[2] * _dot(merged, wo_ref[...])


def _mixer(x, mod, g1, w_in_t, conv_w, conv_b, dt_bias, a_log, d_skip, g_norm,
           wso, wpg, psc, wpo, wo, heads, tile):
    s, d = x.shape
    inner = heads * SSD_HEAD_DIM
    width = conv_w.shape[1]
    pw = psc.shape[1]
    kern = functools.partial(_mixer_kernel, tile=tile, heads=heads)
    hbm = pl.BlockSpec(memory_space=pl.ANY)
    small = lambda a: _resident(a.shape)
    return pl.pallas_call(
        kern,
        grid=(s // tile,),
        in_specs=[_rows(tile, d), small(mod), small(g1), hbm,
                  small(conv_w), small(conv_b), small(dt_bias), small(a_log), small(d_skip), small(g_norm),
                  hbm, small(wpg), small(psc), hbm, hbm],
        out_specs=_rows(tile, d),
        out_shape=jax.ShapeDtypeStruct((s, d), F32),
        scratch_shapes=[pltpu.VMEM((SUBLANES, width), F32),
                        pltpu.VMEM((tile, width), F32),
                        pltpu.VMEM((SSD_STATE, inner), F32),
                        pltpu.VMEM((tile, inner), F32),
                        pltpu.VMEM((tile + POOL_HALO, pw), F32),
                        pltpu.VMEM((tile, inner), F32),
                        pltpu.VMEM((tile, 2 * d), F32),
                        pltpu.VMEM((inner + width, d), BF16),
                        pltpu.VMEM((d, LANES), BF16),
                        pltpu.VMEM((pw, d), BF16),
                        pltpu.VMEM((2 * d, d), BF16),
                        pltpu.VMEM(wso.shape, BF16),
                        pltpu.VMEM(wpo.shape, BF16),
                        pltpu.VMEM(wo.shape, BF16)],
        compiler_params=_params(),
        name="mixer",
    )(x, mod, g1, w_in_t, conv_w, conv_b, dt_bias, a_log, d_skip, g_norm, wso, wpg, psc, wpo, wo)


def _ffn_kernel(h_ref, mod_ref, g2_ref, win_hbm, cw_ref, cb_ref, wout_hbm, gf_ref,
                o_ref, gpad_ref, act_ref, win_ref, wout_ref, *, tile):
    f = wout_ref.shape[0]
    blk = FFN_COLS

    @pl.when(pl.program_id(0) == 0)
    def _():
        gpad_ref[0:SUBLANES, :] = jnp.zeros((SUBLANES, f), F32)
        in_rows = WEIGHT_ROWS // 2
        _stream_cast(win_hbm, in_rows, _cast_into(win_ref, in_rows))
        out_rows = f // 8
        _stream_cast(wout_hbm, out_rows, _cast_into(wout_ref, out_rows))

    h = h_ref[...]
    v = _modulated_rmsnorm(h, g2_ref[...], mod_ref[3], mod_ref[4]).astype(BF16)
    blocks = [slice(c0, min(c0 + blk, f)) for c0 in range(0, f, blk)]

    def gate_proj(cs):
        gpad_ref[SUBLANES:SUBLANES + tile, cs] = _dot(v, win_ref[:, f + cs.start:f + cs.stop])

    def conv_glu(cs):
        gb = gpad_ref[0:tile + SUBLANES, cs]
        acc = cb_ref[:, cs] + cw_ref[FFN_CONV - 1:FFN_CONV, cs] * gb[SUBLANES:]
        for shift in range(1, FFN_CONV):
            k = FFN_CONV - 1 - shift
            acc = acc + cw_ref[k:k + 1, cs] * pltpu.roll(gb, shift, axis=0)[SUBLANES:]
        gpad_ref[0:SUBLANES, cs] = gpad_ref[tile:tile + SUBLANES, cs]
        act_ref[:, cs] = (_silu(acc) * _dot(v, win_ref[:, cs])).astype(BF16)

    for j, cs in enumerate(blocks):
        gate_proj(cs)
        if j > 0:
            conv_glu(blocks[j - 1])
    conv_glu(blocks[-1])
    h2 = h + mod_ref[5] * _dot(act_ref[...], wout_ref[...])
    o_ref[...] = h2 * lax.rsqrt(jnp.mean(h2 * h2, axis=-1, keepdims=True) + EPS) * gf_ref[...]


def _ffn(h, mod, g2, win, conv_w, conv_b, wout, gf, tile):
    s, d = h.shape
    f = wout.shape[0]
    kern = functools.partial(_ffn_kernel, tile=tile)
    hbm = pl.BlockSpec(memory_space=pl.ANY)
    small = lambda a: _resident(a.shape)
    return pl.pallas_call(
        kern,
        grid=(s // tile,),
        in_specs=[_rows(tile, d), small(mod), small(g2), hbm, small(conv_w), small(conv_b), hbm, small(gf)],
        out_specs=_rows(tile, d),
        out_shape=jax.ShapeDtypeStruct((s, d), F32),
        scratch_shapes=[pltpu.VMEM((tile + SUBLANES, f), F32),
                        pltpu.VMEM((tile, f), BF16),
                        pltpu.VMEM(win.shape, BF16),
                        pltpu.VMEM(wout.shape, BF16)],
        compiler_params=_params(),
        name="ffn",
    )(h, mod, g2, win, conv_w, conv_b, wout, gf)


def _layer(h, c, w_ada, b_ada, g_norm1, w_in, ssd_conv_w, ssd_conv_b, ssd_dt_bias, ssd_a_log,
           ssd_d, g_ssd_norm, w_ssd_out, w_pool_grp, pool_scale, w_pool_out, w_out, g_norm2,
           w_ffn_in, ffn_conv_w, ffn_conv_b, w_ffn_out, g_final):
    s, d = h.shape
    heads = ssd_dt_bias.shape[0]
    inner = heads * SSD_HEAD_DIM
    conv_dim = ssd_conv_w.shape[1]
    pool_w = pool_scale.shape[0]
    f = ffn_conv_b.shape[0]
    mod = _ada(c, w_ada, b_ada)

    h1 = _mixer(h, mod, g_norm1.reshape(1, d), w_in.T,
                ssd_conv_w, ssd_conv_b.reshape(1, conv_dim),
                ssd_dt_bias.reshape(1, heads), ssd_a_log.reshape(1, heads),
                jnp.repeat(ssd_d, SSD_HEAD_DIM).reshape(1, inner), g_ssd_norm.reshape(1, inner),
                w_ssd_out, w_pool_grp.astype(BF16), pool_scale.reshape(1, pool_w), w_pool_out, w_out,
                heads, tile=MIXER_TILE)
    return _ffn(h1, mod, g_norm2.reshape(1, d), w_ffn_in, ffn_conv_w, ffn_conv_b.reshape(1, f),
                w_ffn_out, g_final.reshape(1, d), tile=FFN_TILE)


def kernel(x, c, w_ada, b_ada, g_norm1, w_in, ssd_conv_w, ssd_conv_b, ssd_dt_bias, ssd_a_log, ssd_d,
           g_ssd_norm, w_ssd_out, w_pool_grp, pool_scale, w_pool_out, w_out, g_norm2, w_ffn_in,
           ffn_conv_w, ffn_conv_b, w_ffn_out, g_final):
    assert x.shape[0] == 1 and w_ada.shape[0] == 1, "one sequence, one layer"
    out = _layer(x[0], c, w_ada[0], b_ada[0], g_norm1[0], w_in[0], ssd_conv_w[0], ssd_conv_b[0],
                 ssd_dt_bias[0], ssd_a_log[0], ssd_d[0], g_ssd_norm[0], w_ssd_out[0], w_pool_grp[0],
                 pool_scale[0], w_pool_out[0], w_out[0], g_norm2[0], w_ffn_in[0], ffn_conv_w[0],
                 ffn_conv_b[0], w_ffn_out[0], g_final)
    return out[None]
```

```python
import functools

import jax
import jax.numpy as jnp
from jax import lax
from jax.experimental import pallas as pl
from jax.experimental.pallas import tpu as pltpu

F32 = jnp.float32
BF16 = jnp.bfloat16

EPS = 1e-6
LOG2E = 1.4426950408889634
SSD_HEAD_DIM = 64
SSD_GROUPS = 4
SSD_STATE = 128
SSD_CONV = 4
SSD_CHUNK = 128
POOL_WINDOWS = (2, 4, 8, 16)
POOL_HALO = 16
FFN_CONV = 3
LANES = 128
SUBLANES = 8
VMEM_LIMIT = 56 * 1024 * 1024
MIXER_TILE = 256
FFN_TILE = 512
MIXER_COLS = 4 * LANES
FFN_COLS = 2 * LANES
WEIGHT_ROWS = 128
STAGE_SLOTS = 8


def _dot(a, b):
    return jnp.dot(a, b, preferred_element_type=F32)


def _dot_nt(a, b):
    return lax.dot_general(a, b, (((1,), (1,)), ((), ())), preferred_element_type=F32)


def _silu(v):
    return v * jax.nn.sigmoid(v)


def _resident(shape):
    return pl.BlockSpec(shape, lambda i: (0,) * len(shape), pipeline_mode=pl.Buffered(1))


def _rows(tile, width):
    return pl.BlockSpec((tile, width), lambda i: (i, 0))


def _params(vmem=VMEM_LIMIT):
    return pltpu.CompilerParams(dimension_semantics=("arbitrary",), vmem_limit_bytes=vmem)


def _stream_cast(src_hbm, row_chunk, store, row0=0, rows=None):
    rows = src_hbm.shape[0] - row0 if rows is None else rows
    width = src_hbm.shape[1]
    n = rows // row_chunk
    assert n * row_chunk == rows

    def body(stage, sem):
        def copy(k):
            slot = k % STAGE_SLOTS
            src = src_hbm.at[pl.ds(row0 + k * row_chunk, row_chunk), :]
            return pltpu.make_async_copy(src, stage.at[slot], sem.at[slot])
        for k in range(min(STAGE_SLOTS - 1, n)):
            copy(k).start()
        for k in range(n):
            if k + STAGE_SLOTS - 1 < n:
                copy(k + STAGE_SLOTS - 1).start()
            copy(k).wait()
            store(k, stage[k % STAGE_SLOTS])

    pl.run_scoped(body, pltpu.VMEM((STAGE_SLOTS, row_chunk, width), F32),
                  pltpu.SemaphoreType.DMA((STAGE_SLOTS,)))


def _cast_into(dst_ref, row_chunk):
    def store(k, chunk):
        dst_ref[k * row_chunk:(k + 1) * row_chunk, :] = chunk.astype(dst_ref.dtype)
    return store


def _ada_kernel(c_ref, w_hbm, b_ref, o_ref, acc_ref):
    d = c_ref.shape[1]
    cond = _silu(c_ref[...])
    cond8 = jnp.broadcast_to(cond, (SUBLANES, d)).astype(BF16)
    acc_ref[...] = jnp.zeros(acc_ref.shape, F32)

    def store(k, chunk):
        ks = slice(k * WEIGHT_ROWS, (k + 1) * WEIGHT_ROWS)
        acc_ref[...] += _dot(cond8[:, ks], chunk.astype(BF16))
    _stream_cast(w_hbm, WEIGHT_ROWS, store)
    for j in range(o_ref.shape[0]):
        o_ref[j] = acc_ref[0:1, j * d:(j + 1) * d] + b_ref[:, j * d:(j + 1) * d]


def _ada(c, w_ada, b_ada):
    d, n = w_ada.shape
    return pl.pallas_call(
        _ada_kernel,
        grid=(1,),
        in_specs=[_resident((1, d)), pl.BlockSpec(memory_space=pl.ANY), _resident((1, n))],
        out_specs=pl.BlockSpec((n // d, 1, d), lambda i: (0, 0, 0)),
        out_shape=jax.ShapeDtypeStruct((n // d, 1, d), F32),
        scratch_shapes=[pltpu.VMEM((SUBLANES, n), F32)],
        compiler_params=_params(),
        name="ada",
    )(c, w_ada, b_ada.reshape(1, n))


def _modulated_rmsnorm(x, g, shift, scale):
    y = x * lax.rsqrt(jnp.mean(x * x, axis=-1, keepdims=True) + EPS) * g
    return y * (1.0 + scale) + shift


def _split3(v):
    hi = v.astype(BF16)
    r = v - hi.astype(F32)
    mid = r.astype(BF16)
    lo = (r - mid.astype(F32)).astype(BF16)
    return hi, mid, lo


def _conv_silu_block(xblk, halo_ref, xc_ref, cw_ref, cb_ref, cs):
    rows = xblk.shape[0]
    xb = jnp.concatenate([halo_ref[:, cs], xblk], axis=0)
    acc = cb_ref[:, cs] + cw_ref[SSD_CONV - 1:SSD_CONV, cs] * xblk
    for shift in range(1, SSD_CONV):
        k = SSD_CONV - 1 - shift
        acc = acc + cw_ref[k:k + 1, cs] * pltpu.roll(xb, shift, axis=0)[SUBLANES:]
    xc_ref[:, cs] = _silu(acc)
    halo_ref[:, cs] = xblk[rows - SUBLANES:]


def _ssd_chunk(xc_ref, r0, dtv, alog, dsk_ref, st_ref, y_ref, fillers, *, inner, heads):
    L = SSD_CHUNK
    N = SSD_STATE
    P = SSD_HEAD_DIM
    hpg = heads // SSD_GROUPS
    rs = slice(r0, r0 + L)

    da = dtv * (-jnp.exp(alog))
    row = lax.broadcasted_iota(jnp.int32, (L, L), 0)
    col = lax.broadcasted_iota(jnp.int32, (L, L), 1)
    causal = row >= col
    tril = jnp.where(causal, 1.0, 0.0).astype(BF16)
    zero_b = jnp.zeros((L, L), BF16)
    hi, mid, lo = _split3(da)
    acum = (_dot(tril, hi) + _dot(tril, mid) + _dot(tril, lo)) * LOG2E
    src_t = (acum - jnp.log2(dtv)).T
    last = acum[L - 1:L, :]
    wgt = dtv * jnp.exp2(last - acum)
    ea_last = jnp.exp2(last)

    lane = lax.broadcasted_iota(jnp.int32, (L, 2 * LANES), 1)
    lane1 = lax.broadcasted_iota(jnp.int32, (L, LANES), 1)
    lane_row = lax.broadcasted_iota(jnp.int32, (1, LANES), 1)

    for g in range(SSD_GROUPS):
        b_g = xc_ref[rs, inner + g * N: inner + (g + 1) * N]
        c_g = xc_ref[rs, inner + (SSD_GROUPS + g) * N: inner + (SSD_GROUPS + g + 1) * N]
        c_b = c_g.astype(BF16)
        cb_b = lax.dot_general(c_b, b_g.astype(BF16),
                               (((1,), (1,)), ((), ())), preferred_element_type=F32).astype(BF16)
        scale_rows = []
        for quad in range(hpg // 4):
            h0 = g * hpg + quad * 4
            q0 = h0 * P
            lhs = []
            for h in range(h0, h0 + 4):
                colb = jnp.broadcast_to(acum[:, h:h + 1], (L, L))
                w_h = cb_b * jnp.where(causal, jnp.exp2(colb - src_t[h:h + 1, :]).astype(BF16), zero_b)
                e_h = c_b * jnp.exp2(colb).astype(BF16)
                lhs.append(jnp.concatenate([w_h, e_h], axis=1))
                scale_rows.append(jnp.broadcast_to(ea_last[:, h:h + 1], (1, LANES)))
            lhs = jnp.concatenate(lhs, axis=0)
            xq = xc_ref[rs, q0:q0 + 4 * P]
            rhs = jnp.concatenate([xq.astype(BF16), st_ref[:, q0:q0 + 4 * P].astype(BF16)], axis=0)
            out = _dot(lhs, rhs)
            yq = jnp.where(lane < P, out[0:L],
                           jnp.where(lane < 2 * P, out[L:2 * L],
                                     jnp.where(lane < 3 * P, out[2 * L:3 * L], out[3 * L:4 * L])))
            y_ref[rs, q0:q0 + 4 * P] = yq + dsk_ref[:, q0:q0 + 4 * P] * xq
            next(fillers, lambda: None)()
        xw = []
        for pair in range(hpg // 2):
            h0 = g * hpg + pair * 2
            p0 = h0 * P
            wb = jnp.where(lane1 < P,
                           jnp.broadcast_to(wgt[:, h0:h0 + 1], (L, LANES)),
                           jnp.broadcast_to(wgt[:, h0 + 1:h0 + 2], (L, LANES)))
            xw.append((xc_ref[rs, p0:p0 + 2 * P] * wb).astype(BF16))
            srow = jnp.where(lane_row < P, scale_rows[pair * 2], scale_rows[pair * 2 + 1])
            st_ref[:, p0:p0 + 2 * P] = st_ref[:, p0:p0 + 2 * P] * srow
        xw = jnp.concatenate(xw, axis=1)
        g0 = g * hpg * P
        st_ref[:, g0:g0 + hpg * P] += _dot(b_g.T.astype(BF16), xw)


def _pool_mix(pext_ref, wpg_ref, psc_ref, tile, step_idx):
    pw = pext_ref.shape[1]
    gd = pw // len(POOL_WINDOWS)
    t_glob = step_idx * tile + lax.broadcasted_iota(jnp.int32, (tile, 1), 0) + 1
    pooled = []
    for g, w in enumerate(POOL_WINDOWS):
        gs = slice(g * gd, (g + 1) * gd)
        acc = pext_ref[:, gs]
        step = 1
        while step < w:
            acc = acc + pltpu.roll(acc, step, axis=0)
            step *= 2
        inv_cnt = 1.0 / jnp.minimum(t_glob, w).astype(F32)
        pg = acc[POOL_HALO:, :] * inv_cnt - pext_ref[POOL_HALO:POOL_HALO + tile, gs]
        pooled.append(_dot(pg.astype(BF16), wpg_ref[g]))
    pext_ref[0:POOL_HALO, :] = pext_ref[tile:tile + POOL_HALO, :]
    return (jnp.concatenate(pooled, axis=1) * psc_ref[...]).astype(BF16)


def _mixer_kernel(x_ref, mod_ref, g1_ref, w_in_t_hbm,
                  cw_ref, cb_ref, dtb_ref, alog_ref, dsk_ref, gn_ref,
                  wso_hbm, wpg_ref, psc_ref, wpo_hbm, wo_hbm,
                  h_ref, halo_ref, xc_ref, st_ref, y_ref, pext_ref, z_ref, gt_ref,
                  wzx_ref, wdt_ref, wp_ref, wg_ref, wso_ref, wpo_ref, wo_ref, *, tile, heads):
    d = x_ref.shape[1]
    inner = y_ref.shape[1]
    L = SSD_CHUNK
    i = pl.program_id(0)

    @pl.when(i == 0)
    def _():
        halo_ref[...] = jnp.zeros(halo_ref.shape, F32)
        st_ref[...] = jnp.zeros(st_ref.shape, F32)
        pext_ref[0:POOL_HALO, :] = jnp.zeros((POOL_HALO, pext_ref.shape[1]), F32)
        chunk_rows = 2 * WEIGHT_ROWS
        n_zx, pw, gw2 = wzx_ref.shape[0], wp_ref.shape[0], wg_ref.shape[0]
        _stream_cast(w_in_t_hbm, chunk_rows, _cast_into(wzx_ref, chunk_rows), 0, n_zx)

        def store_dt(k, chunk):
            lane = lax.broadcasted_iota(jnp.int32, (chunk.shape[1], LANES), 1)
            wdt_ref[...] = jnp.where(lane < heads, chunk.T, 0.0).astype(BF16)
        _stream_cast(w_in_t_hbm, LANES, store_dt, n_zx, LANES)
        _stream_cast(w_in_t_hbm, chunk_rows, _cast_into(wp_ref, chunk_rows), n_zx + heads, pw)
        _stream_cast(w_in_t_hbm, chunk_rows, _cast_into(wg_ref, chunk_rows), n_zx + heads + pw, gw2)
        for src, dst in ((wso_hbm, wso_ref), (wpo_hbm, wpo_ref), (wo_hbm, wo_ref)):
            _stream_cast(src, chunk_rows, _cast_into(dst, chunk_rows))

    x = x_ref[...]
    u = _modulated_rmsnorm(x, g1_ref[...], mod_ref[0], mod_ref[1]).astype(BF16)

    blk = MIXER_COLS

    def proj(dst_ref, r_off, w_ref, w_off, cs):
        def run():
            dst_ref[r_off:r_off + tile, cs] = _dot_nt(u, w_ref[w_off + cs.start:w_off + cs.stop, :])
        return run

    for c0 in range(0, xc_ref.shape[1], blk):
        xblk = _dot_nt(u, wzx_ref[inner + c0:inner + c0 + blk, :])
        _conv_silu_block(xblk, halo_ref, xc_ref, cw_ref, cb_ref, slice(c0, c0 + blk))
    pad = jnp.zeros((1, LANES - heads), F32)
    dtb = jnp.concatenate([dtb_ref[...], pad], axis=1)
    alog = jnp.concatenate([alog_ref[...], pad], axis=1)
    dtv = jax.nn.softplus(_dot(u, wdt_ref[...]) + dtb)

    fillers = iter([proj(z_ref, 0, wzx_ref, 0, slice(c0, c0 + blk)) for c0 in range(0, inner, blk)]
                   + [proj(pext_ref, POOL_HALO, wp_ref, 0, slice(c0, c0 + blk))
                      for c0 in range(0, pext_ref.shape[1], blk)]
                   + [proj(gt_ref, 0, wg_ref, 0, slice(c0, c0 + blk)) for c0 in range(0, 2 * d, blk)])
    for c in range(tile // L):
        _ssd_chunk(xc_ref, c * L, dtv[c * L:(c + 1) * L], alog, dsk_ref, st_ref, y_ref, fillers,
                   inner=inner, heads=heads)
    for run in fillers:
        run()

    gw = inner // SSD_GROUPS
    yn = []
    for g in range(SSD_GROUPS):
        gs = slice(g * gw, (g + 1) * gw)
        t = y_ref[:, gs] * _silu(z_ref[:, gs])
        ms = jnp.mean(t * t, axis=-1, keepdims=True)
        yn.append((t * lax.rsqrt(ms + EPS) * gn_ref[:, gs]).astype(BF16))
    y_a = _dot(jnp.concatenate(yn, axis=1), wso_ref[...])

    pb = _pool_mix(pext_ref, wpg_ref, psc_ref, tile, i)
    y_b = _dot(pb, wpo_ref[...])

    sg = jax.nn.sigmoid(gt_ref[...])
    merged = (sg[:, :d] * y_a + sg[:, d:] * y_b).astype(BF16)
    h_ref[...] = x + mod_ref[2] * _dot(merged, wo_ref[...])


def _mixer(x, mod, g1, w_in_t, conv_w, conv_b, dt_bias, a_log, d_skip, g_norm,
           wso, wpg, psc, wpo, wo, heads, tile):
    s, d = x.shape
    inner = heads * SSD_HEAD_DIM
    width = conv_w.shape[1]
    pw = psc.shape[1]
    kern = functools.partial(_mixer_kernel, tile=tile, heads=heads)
    hbm = pl.BlockSpec(memory_space=pl.ANY)
    small = lambda a: _resident(a.shape)
    return pl.pallas_call(
        kern,
        grid=(s // tile,),
        in_specs=[_rows(tile, d), small(mod), small(g1), hbm,
                  small(conv_w), small(conv_b), small(dt_bias), small(a_log), small(d_skip), small(g_norm),
                  hbm, small(wpg), small(psc), hbm, hbm],
        out_specs=_rows(tile, d),
        out_shape=jax.ShapeDtypeStruct((s, d), F32),
        scratch_shapes=[pltpu.VMEM((SUBLANES, width), F32),
                        pltpu.VMEM((tile, width), F32),
                        pltpu.VMEM((SSD_STATE, inner), F32),
                        pltpu.VMEM((tile, inner), F32),
                        pltpu.VMEM((tile + POOL_HALO, pw), F32),
                        pltpu.VMEM((tile, inner), F32),
                        pltpu.VMEM((tile, 2 * d), F32),
                        pltpu.VMEM((inner + width, d), BF16),
                        pltpu.VMEM((d, LANES), BF16),
                        pltpu.VMEM((pw, d), BF16),
                        pltpu.VMEM((2 * d, d), BF16),
                        pltpu.VMEM(wso.shape, BF16),
                        pltpu.VMEM(wpo.shape, BF16),
                        pltpu.VMEM(wo.shape, BF16)],
        compiler_params=_params(),
        name="mixer",
    )(x, mod, g1, w_in_t, conv_w, conv_b, dt_bias, a_log, d_skip, g_norm, wso, wpg, psc, wpo, wo)


def _ffn_kernel(h_ref, mod_ref, g2_ref, win_hbm, cw_ref, cb_ref, wout_hbm, gf_ref,
                o_ref, gpad_ref, act_ref, win_ref, wout_ref, *, tile):
    f = wout_ref.shape[0]
    blk = FFN_COLS

    @pl.when(pl.program_id(0) == 0)
    def _():
        gpad_ref[0:SUBLANES, :] = jnp.zeros((SUBLANES, f), F32)
        in_rows = WEIGHT_ROWS // 2
        _stream_cast(win_hbm, in_rows, _cast_into(win_ref, in_rows))
        out_rows = f // 8
        _stream_cast(wout_hbm, out_rows, _cast_into(wout_ref, out_rows))

    h = h_ref[...]
    v = _modulated_rmsnorm(h, g2_ref[...], mod_ref[3], mod_ref[4]).astype(BF16)
    blocks = [slice(c0, min(c0 + blk, f)) for c0 in range(0, f, blk)]

    def gate_proj(cs):
        gpad_ref[SUBLANES:SUBLANES + tile, cs] = _dot(v, win_ref[:, f + cs.start:f + cs.stop])

    def conv_glu(cs):
        gb = gpad_ref[0:tile + SUBLANES, cs]
        acc = cb_ref[:, cs] + cw_ref[FFN_CONV - 1:FFN_CONV, cs] * gb[SUBLANES:]
        for shift in range(1, FFN_CONV):
            k = FFN_CONV - 1 - shift
            acc = acc + cw_ref[k:k + 1, cs] * pltpu.roll(gb, shift, axis=0)[SUBLANES:]
        gpad_ref[0:SUBLANES, cs] = gpad_ref[tile:tile + SUBLANES, cs]
        act_ref[:, cs] = (_silu(acc) * _dot(v, win_ref[:, cs])).astype(BF16)

    for j, cs in enumerate(blocks):
        gate_proj(cs)
        if j > 0:
            conv_glu(blocks[j - 1])
    conv_glu(blocks[-1])
    h2 = h + mod_ref[5] * _dot(act_ref[...], wout_ref[...])
    o_ref[...] = h2 * lax.rsqrt(jnp.mean(h2 * h2, axis=-1, keepdims=True) + EPS) * gf_ref[...]


def _ffn(h, mod, g2, win, conv_w, conv_b, wout, gf, tile):
    s, d = h.shape
    f = wout.shape[0]
    kern = functools.partial(_ffn_kernel, tile=tile)
    hbm = pl.BlockSpec(memory_space=pl.ANY)
    small = lambda a: _resident(a.shape)
    return pl.pallas_call(
        kern,
        grid=(s // tile,),
        in_specs=[_rows(tile, d), small(mod), small(g2), hbm, small(conv_w), small(conv_b), hbm, small(gf)],
        out_specs=_rows(tile, d),
        out_shape=jax.ShapeDtypeStruct((s, d), F32),
        scratch_shapes=[pltpu.VMEM((tile + SUBLANES, f), F32),
                        pltpu.VMEM((tile, f), BF16),
                        pltpu.VMEM(win.shape, BF16),
                        pltpu.VMEM(wout.shape, BF16)],
        compiler_params=_params(),
        name="ffn",
    )(h, mod, g2, win, conv_w, conv_b, wout, gf)


def _layer(h, c, w_ada, b_ada, g_norm1, w_in, ssd_conv_w, ssd_conv_b, ssd_dt_bias, ssd_a_log,
           ssd_d, g_ssd_norm, w_ssd_out, w_pool_grp, pool_scale, w_pool_out, w_out, g_norm2,
           w_ffn_in, ffn_conv_w, ffn_conv_b, w_ffn_out, g_final):
    s, d = h.shape
    heads = ssd_dt_bias.shape[0]
    inner = heads * SSD_HEAD_DIM
    conv_dim = ssd_conv_w.shape[1]
    pool_w = pool_scale.shape[0]
    f = ffn_conv_b.shape[0]
    mod = _ada(c, w_ada, b_ada)

    h1 = _mixer(h, mod, g_norm1.reshape(1, d), w_in.T,
                ssd_conv_w, ssd_conv_b.reshape(1, conv_dim),
                ssd_dt_bias.reshape(1, heads), ssd_a_log.reshape(1, heads),
                jnp.repeat(ssd_d, SSD_HEAD_DIM).reshape(1, inner), g_ssd_norm.reshape(1, inner),
                w_ssd_out, w_pool_grp.astype(BF16), pool_scale.reshape(1, pool_w), w_pool_out, w_out,
                heads, tile=MIXER_TILE)
    return _ffn(h1, mod, g_norm2.reshape(1, d), w_ffn_in, ffn_conv_w, ffn_conv_b.reshape(1, f),
                w_ffn_out, g_final.reshape(1, d), tile=FFN_TILE)


def kernel(x, c, w_ada, b_ada, g_norm1, w_in, ssd_conv_w, ssd_conv_b, ssd_dt_bias, ssd_a_log, ssd_d,
           g_ssd_norm, w_ssd_out, w_pool_grp, pool_scale, w_pool_out, w_out, g_norm2, w_ffn_in,
           ffn_conv_w, ffn_conv_b, w_ffn_out, g_final):
    assert x.shape[0] == 1 and w_ada.shape[0] == 1, "one sequence, one layer"
    out = _layer(x[0], c, w_ada[0], b_ada[0], g_norm1[0], w_in[0], ssd_conv_w[0], ssd_conv_b[0],
                 ssd_dt_bias[0], ssd_a_log[0], ssd_d[0], g_ssd_norm[0], w_ssd_out[0], w_pool_grp[0],
                 pool_scale[0], w_pool_out[0], w_out[0], g_norm2[0], w_ffn_in[0], ffn_conv_w[0],
                 ffn_conv_b[0], w_ffn_out[0], g_final)
    return out[None]
```
